```python
import jax, jax.numpy as jnp
from jax import lax
import numpy as np

D_MODEL = 2048
BATCH = 4
SEQ = 8192
DEPTH = 1

D_MIX = D_MODEL
GM_WIDTH = D_MIX // 2
LRU_WIDTH = D_MIX - GM_WIDTH
CHUNK = 128
GM_HEADS = 8
GM_HEAD_DIM = GM_WIDTH // GM_HEADS
LRU_HEADS = 8
LRU_BLOCK = LRU_WIDTH // LRU_HEADS
LRU_CONV = 4
LRU_C = 8.0
FFN_MULT = 3
D_FF = FFN_MULT * D_MODEL
FFN_CONV = 3
IN_COLS = 2 * GM_WIDTH + 2 * LRU_WIDTH
RMS_EPS = 1e-6
LN_EPS = 1e-5

kernel_name = "hymba_style_gmlp_rglru_convffn"


def _rmsnorm(x, g):
    xf = x.astype(jnp.float32)
    y = xf * lax.rsqrt(jnp.mean(xf * xf, axis=-1, keepdims=True) + RMS_EPS)
    return (y * g.astype(jnp.float32)).astype(x.dtype)


def _layernorm(x, g, b):
    xf = x.astype(jnp.float32)
    mu = jnp.mean(xf, axis=-1, keepdims=True)
    xc = xf - mu
    y = xc * lax.rsqrt(jnp.mean(xc * xc, axis=-1, keepdims=True) + LN_EPS)
    return (y * g.astype(jnp.float32) + b.astype(jnp.float32)).astype(x.dtype)


def _causal_dwconv(x, w, b):
    k_width = w.shape[0]
    s = x.shape[1]
    xp = jnp.pad(x, ((0, 0), (k_width - 1, 0), (0, 0)))
    y = b
    for k in range(k_width):
        y = y + xp[:, k:k + s] * w[k]
    return y


def _spatial_gating(z, v_g, v_b, ws, bs):
    u, v = jnp.split(z, 2, axis=-1)
    v = _layernorm(v, v_g, v_b)
    bsz, s, _ = v.shape
    vc = v.reshape(bsz, s // CHUNK, CHUNK, GM_HEADS, GM_HEAD_DIM)
    mask = jnp.tril(jnp.ones((CHUNK, CHUNK), dtype=bool))
    w = jnp.where(mask[None], ws, jnp.zeros((), ws.dtype))
    mixed = jnp.einsum('hts,bcshd->bcthd', w, vc) + bs.T[None, None, :, :, None]
    return u * mixed.reshape(bsz, s, GM_WIDTH)


def _lru_combine(left, right):
    a1, b1 = left
    a2, b2 = right
    return a1 * a2, a2 * b1 + b2


def _rg_lru(x, wa, ba, wx, bx, lam):
    bsz, s, w = x.shape
    xh = x.reshape(bsz, s, LRU_HEADS, LRU_BLOCK)
    r = jax.nn.sigmoid(jnp.einsum('bshi,hij->bshj', xh, wa) + ba).reshape(bsz, s, w)
    i = jax.nn.sigmoid(jnp.einsum('bshi,hij->bshj', xh, wx) + bx).reshape(bsz, s, w)
    log_a = -LRU_C * r.astype(jnp.float32) * jax.nn.softplus(-lam.astype(jnp.float32))
    a = jnp.exp(log_a)
    mult = jnp.sqrt(-jnp.expm1(2.0 * log_a))
    b = mult * (i * x).astype(jnp.float32)
    _, h = lax.associative_scan(_lru_combine, (a, b), axis=1)
    return h.astype(x.dtype)


def setup_inputs(seed: int = 0) -> dict:
    key = jax.random.key(seed)
    ks = jax.random.split(key, 24)
    f32 = jnp.float32
    L = DEPTH

    def nrm(k, shape, scale):
        return jax.random.normal(k, shape, f32) * scale

    x = jax.random.normal(ks[0], (BATCH, SEQ, D_MODEL), f32)
    norm1_g = 1.0 + nrm(ks[1], (L, D_MODEL), 0.05)
    w_in = nrm(ks[2], (L, D_MODEL, IN_COLS), D_MODEL ** -0.5)
    gm_v_g = 1.0 + nrm(ks[3], (L, GM_WIDTH), 0.05)
    gm_v_b = nrm(ks[4], (L, GM_WIDTH), 0.02)
    gm_ws = nrm(ks[5], (L, GM_HEADS, CHUNK, CHUNK), CHUNK ** -0.5)
    gm_bs = 1.0 + nrm(ks[6], (L, GM_HEADS, CHUNK), 0.1)
    lru_conv_w = nrm(ks[7], (L, LRU_CONV, LRU_WIDTH), LRU_CONV ** -0.5)
    lru_conv_b = nrm(ks[8], (L, LRU_WIDTH), 0.02)
    lru_wa = nrm(ks[9], (L, LRU_HEADS, LRU_BLOCK, LRU_BLOCK), LRU_BLOCK ** -0.5)
    lru_ba = nrm(ks[10], (L, LRU_HEADS, LRU_BLOCK), 0.02)
    lru_wx = nrm(ks[11], (L, LRU_HEADS, LRU_BLOCK, LRU_BLOCK), LRU_BLOCK ** -0.5)
    lru_bx = nrm(ks[12], (L, LRU_HEADS, LRU_BLOCK), 0.02)
    a_c = jax.random.uniform(ks[13], (L, LRU_WIDTH), f32, 0.9, 0.999)
    a_base = a_c ** (1.0 / LRU_C)
    lru_lambda = jnp.log(a_base) - jnp.log1p(-a_base)
    gm_out_g = 1.0 + nrm(ks[14], (L, GM_WIDTH), 0.05)
    lru_out_g = 1.0 + nrm(ks[15], (L, LRU_WIDTH), 0.05)
    w_out = nrm(ks[16], (L, D_MIX, D_MODEL), D_MIX ** -0.5)
    norm2_g = 1.0 + nrm(ks[17], (L, D_MODEL), 0.05)
    ffn_w_up = nrm(ks[18], (L, D_MODEL, 2 * D_FF), D_MODEL ** -0.5)
    ffn_conv_w = nrm(ks[19], (L, FFN_CONV, 2 * D_FF), FFN_CONV ** -0.5)
    ffn_conv_b = nrm(ks[20], (L, 2 * D_FF), 0.02)
    ffn_w_down = nrm(ks[21], (L, D_FF, D_MODEL), D_FF ** -0.5)
    final_g = 1.0 + nrm(ks[22], (D_MODEL,), 0.05)
    return {
        "x": x, "norm1_g": norm1_g, "w_in": w_in,
        "gm_v_g": gm_v_g, "gm_v_b": gm_v_b, "gm_ws": gm_ws, "gm_bs": gm_bs,
        "lru_conv_w": lru_conv_w, "lru_conv_b": lru_conv_b,
        "lru_wa": lru_wa, "lru_ba": lru_ba, "lru_wx": lru_wx, "lru_bx": lru_bx,
        "lru_lambda": lru_lambda, "gm_out_g": gm_out_g, "lru_out_g": lru_out_g,
        "w_out": w_out, "norm2_g": norm2_g, "ffn_w_up": ffn_w_up,
        "ffn_conv_w": ffn_conv_w, "ffn_conv_b": ffn_conv_b, "ffn_w_down": ffn_w_down,
        "final_g": final_g,
    }


def reference(x, norm1_g, w_in, gm_v_g, gm_v_b, gm_ws, gm_bs, lru_conv_w, lru_conv_b,
              lru_wa, lru_ba, lru_wx, lru_bx, lru_lambda, gm_out_g, lru_out_g, w_out,
              norm2_g, ffn_w_up, ffn_conv_w, ffn_conv_b, ffn_w_down, final_g):
    for l in range(DEPTH):
        h = _rmsnorm(x, norm1_g[l])
        p = jnp.einsum('bsd,de->bse', h, w_in[l])
        z_gm = p[..., :2 * GM_WIDTH]
        g_lru = p[..., 2 * GM_WIDTH:2 * GM_WIDTH + LRU_WIDTH]
        x_lru = p[..., 2 * GM_WIDTH + LRU_WIDTH:]
        y_gm = _spatial_gating(jax.nn.gelu(z_gm), gm_v_g[l], gm_v_b[l], gm_ws[l], gm_bs[l])
        xr = _causal_dwconv(x_lru, lru_conv_w[l], lru_conv_b[l])
        y_lru = _rg_lru(xr, lru_wa[l], lru_ba[l], lru_wx[l], lru_bx[l], lru_lambda[l])
        y_lru = y_lru * jax.nn.gelu(g_lru)
        y = jnp.concatenate([_rmsnorm(y_gm, gm_out_g[l]), _rmsnorm(y_lru, lru_out_g[l])], axis=-1)
        x = x + jnp.einsum('bse,ed->bsd', y, w_out[l])
        h = _rmsnorm(x, norm2_g[l])
        up = jnp.einsum('bsd,df->bsf', h, ffn_w_up[l])
        up = _causal_dwconv(up, ffn_conv_w[l], ffn_conv_b[l])
        gate, val = jnp.split(up, 2, axis=-1)
        x = x + jnp.einsum('bsf,fd->bsd', jax.nn.gelu(gate) * val, ffn_w_down[l])
    return _rmsnorm(x, final_g)
```

```python
import functools

import jax
import jax.numpy as jnp
from jax import lax
from jax.experimental import pallas as pl
from jax.experimental.pallas import tpu as pltpu

D_MODEL = 2048
GM_WIDTH = 1024
LRU_WIDTH = 1024
CHUNK = 128
HEADS = 8
HEAD_DIM = 128
LRU_CONV = 4
LRU_C = 8.0
D_FF = 6144
FFN_CONV = 3
IN_COLS = 2 * GM_WIDTH + 2 * LRU_WIDTH
RMS_EPS = 1e-6
LN_EPS = 1e-5

SUBLANES = 8
HALO = SUBLANES
VMEM_LIMIT_BYTES = 56 * 1024 * 1024

MIX_TM = 256
FFN_TM = 512
FFN_TF = 512
N_F = D_FF // FFN_TF

_BF16 = jnp.bfloat16
_F32 = jnp.float32


def _rms(x, g):
    return x * lax.rsqrt(jnp.mean(x * x, axis=-1, keepdims=True) + RMS_EPS) * g


def _dot(a, b):
    return jnp.dot(a, b, preferred_element_type=_F32)


def _mixer_kernel(x_ref, n1g_ref, w_in_ref, vg_ref, vb_ref, ws_ref, bsf_ref,
                  cw_ref, cb_ref, wgate_ref, bgate_ref, lam_ref, gog_ref, log_ref,
                  w_out_ref, o_ref,
                  xl_scr, mix_scr, a_scr, b_scr, h_scr, hcarry_scr):
    tm = MIX_TM
    j = pl.program_id(1)

    @pl.when(j == 0)
    def _():
        xl_scr[0:HALO, :] = jnp.zeros((HALO, LRU_WIDTH), _F32)
        hcarry_scr[...] = jnp.zeros_like(hcarry_scr)

    x = x_ref[...]
    hb = _rms(x, n1g_ref[...]).astype(_BF16)

    u = jax.nn.gelu(_dot(hb, w_in_ref[:, 0:GM_WIDTH]))
    v = jax.nn.gelu(_dot(hb, w_in_ref[:, GM_WIDTH:2 * GM_WIDTH]))
    mu = jnp.mean(v, axis=-1, keepdims=True)
    vc = v - mu
    v = vc * lax.rsqrt(jnp.mean(vc * vc, axis=-1, keepdims=True) + LN_EPS)
    vbf = (v * vg_ref[...] + vb_ref[...]).astype(_BF16)
    tri = (lax.broadcasted_iota(jnp.int32, (CHUNK, CHUNK), 0)
           >= lax.broadcasted_iota(jnp.int32, (CHUNK, CHUNK), 1))
    for h in range(HEADS):
        hs = slice(h * HEAD_DIM, (h + 1) * HEAD_DIM)
        w_h = jnp.where(tri, ws_ref[h], 0.0).astype(_BF16)
        for c in range(0, tm // CHUNK, 2):
            v2 = jnp.concatenate(
                [vbf[c * CHUNK:(c + 1) * CHUNK, hs], vbf[(c + 1) * CHUNK:(c + 2) * CHUNK, hs]],
                axis=1)
            m = _dot(w_h, v2)
            mix_scr[c * CHUNK:(c + 1) * CHUNK, hs] = m[:, :HEAD_DIM] + bsf_ref[:, hs]
            mix_scr[(c + 1) * CHUNK:(c + 2) * CHUNK, hs] = m[:, HEAD_DIM:] + bsf_ref[:, hs]
    y_gm = _rms(u * mix_scr[...], gog_ref[...]).astype(_BF16)

    g_lru = _dot(hb, w_in_ref[:, 2 * GM_WIDTH:2 * GM_WIDTH + LRU_WIDTH])
    x_lru = _dot(hb, w_in_ref[:, 2 * GM_WIDTH + LRU_WIDTH:])
    xl_scr[HALO:HALO + tm, :] = x_lru
    xr = cb_ref[...] + cw_ref[LRU_CONV - 1:LRU_CONV, :] * x_lru
    for k in range(LRU_CONV - 1):
        off = HALO - (LRU_CONV - 1) + k
        xr = xr + cw_ref[k:k + 1, :] * xl_scr[off:off + tm, :]
    xl_scr[0:HALO, :] = x_lru[tm - HALO:tm, :]

    sp = jax.nn.softplus(-lam_ref[...])
    for h in range(HEADS):
        hs = slice(h * HEAD_DIM, (h + 1) * HEAD_DIM)
        xh = xr[:, hs]
        gz = _dot(xh.astype(_BF16), wgate_ref[h]) + bgate_ref[h]
        r = jax.nn.sigmoid(gz[:, :HEAD_DIM])
        i = jax.nn.sigmoid(gz[:, HEAD_DIM:])
        log_a = -LRU_C * r * sp[:, hs]
        a_scr[:, hs] = jnp.exp(log_a)
        t = jnp.tanh(log_a)
        b_scr[:, hs] = jnp.sqrt(-2.0 * t / (1.0 - t)) * (i * xh)

    row = lax.broadcasted_iota(jnp.int32, (SUBLANES, LRU_WIDTH), 0)

    def scan_group(g, h_prev):
        r0 = pl.multiple_of(g * SUBLANES, SUBLANES)
        a = a_scr[pl.ds(r0, SUBLANES), :]
        b = b_scr[pl.ds(r0, SUBLANES), :]
        for s in (1, 2, 4):
            keep = row >= s
            b = jnp.where(keep, a * pltpu.roll(b, s, 0) + b, b)
            a = jnp.where(keep, a * pltpu.roll(a, s, 0), a)
        hh = a * h_prev + b
        h_scr[pl.ds(r0, SUBLANES), :] = hh
        return hh[SUBLANES - 1:SUBLANES, :]

    hcarry_scr[...] = lax.fori_loop(0, tm // SUBLANES, scan_group, hcarry_scr[...])

    y_lru = _rms(h_scr[...] * jax.nn.gelu(g_lru), log_ref[...]).astype(_BF16)

    y = jnp.concatenate([y_gm, y_lru], axis=1)
    o_ref[...] = x + _dot(y, w_out_ref[...])


def _ffn_kernel(x_ref, n2g_ref, wg_ref, wv_ref, cwg_ref, cwv_ref, cbg_ref, cbv_ref,
                wd_ref, fg_ref, o_ref,
                h_scr, ug_scr, uv_scr, carry_g, carry_v):
    tm = FFN_TM
    j = pl.program_id(1)
    f = pl.program_id(2)

    @pl.when(f == 0)
    def _():
        x = x_ref[...]
        h_scr[...] = _rms(x, n2g_ref[...]).astype(_BF16)
        o_ref[...] = x

    @pl.when(j == 0)
    def _():
        carry_g[f] = jnp.zeros((HALO, FFN_TF), _F32)
        carry_v[f] = jnp.zeros((HALO, FFN_TF), _F32)

    hb = h_scr[...]

    def conv(u, scr, carry, cw_ref, cb_ref):
        scr[0:HALO, :] = carry[f]
        scr[HALO:HALO + tm, :] = u
        carry[f] = u[tm - HALO:tm, :]
        y = cb_ref[...] + cw_ref[FFN_CONV - 1:FFN_CONV, :] * u
        for k in range(FFN_CONV - 1):
            off = HALO - (FFN_CONV - 1) + k
            y = y + cw_ref[k:k + 1, :] * scr[off:off + tm, :]
        return y

    gate = conv(_dot(hb, wg_ref[...]), ug_scr, carry_g, cwg_ref, cbg_ref)
    val = conv(_dot(hb, wv_ref[...]), uv_scr, carry_v, cwv_ref, cbv_ref)
    act = (jax.nn.gelu(gate) * val).astype(_BF16)
    o_ref[...] += _dot(act, wd_ref[...])

    @pl.when(f == N_F - 1)
    def _():
        o_ref[...] = _rms(o_ref[...], fg_ref[...])


def _const_spec(shape):
    nd = len(shape)
    return pl.BlockSpec(shape, lambda *_: (0,) * nd, pipeline_mode=pl.Buffered(1))


def _mixer(x2, n1g, w_in, vg, vb, ws, bsf, cw, cb, wgate, bgate, lam, gog, log, w_out,
           batch, seq):
    tm = MIX_TM
    nt = seq // tm
    row_spec = pl.BlockSpec((tm, D_MODEL), lambda b, j: (b * nt + j, 0))
    consts = (n1g, w_in, vg, vb, ws, bsf, cw, cb, wgate, bgate, lam, gog, log, w_out)
    return pl.pallas_call(
        _mixer_kernel,
        grid=(batch, nt),
        in_specs=[row_spec] + [_const_spec(c.shape) for c in consts],
        out_specs=row_spec,
        out_shape=jax.ShapeDtypeStruct(x2.shape, _F32),
        scratch_shapes=[
            pltpu.VMEM((HALO + tm, LRU_WIDTH), _F32),
            pltpu.VMEM((tm, GM_WIDTH), _F32),
            pltpu.VMEM((tm, LRU_WIDTH), _F32),
            pltpu.VMEM((tm, LRU_WIDTH), _F32),
            pltpu.VMEM((tm, LRU_WIDTH), _F32),
            pltpu.VMEM((1, LRU_WIDTH), _F32),
        ],
        compiler_params=pltpu.CompilerParams(
            dimension_semantics=("arbitrary", "arbitrary"),
            vmem_limit_bytes=VMEM_LIMIT_BYTES),
        name="mixer",
    )(x2, *consts)


def _ffn(x2, n2g, w_up, cw, cb, w_down, fg, batch, seq):
    tm, tf = FFN_TM, FFN_TF
    nt = seq // tm
    row_spec = pl.BlockSpec((tm, D_MODEL), lambda b, j, f: (b * nt + j, 0))
    vec_spec = pl.BlockSpec((1, D_MODEL), lambda b, j, f: (0, 0))
    return pl.pallas_call(
        _ffn_kernel,
        grid=(batch, nt, N_F),
        in_specs=[
            row_spec,
            vec_spec,
            pl.BlockSpec((D_MODEL, tf), lambda b, j, f: (0, f)),
            pl.BlockSpec((D_MODEL, tf), lambda b, j, f: (0, N_F + f)),
            pl.BlockSpec((FFN_CONV, tf), lambda b, j, f: (0, f)),
            pl.BlockSpec((FFN_CONV, tf), lambda b, j, f: (0, N_F + f)),
            pl.BlockSpec((1, tf), lambda b, j, f: (0, f)),
            pl.BlockSpec((1, tf), lambda b, j, f: (0, N_F + f)),
            pl.BlockSpec((tf, D_MODEL), lambda b, j, f: (f, 0)),
            vec_spec,
        ],
        out_specs=row_spec,
        out_shape=jax.ShapeDtypeStruct(x2.shape, _F32),
        scratch_shapes=[
            pltpu.VMEM((tm, D_MODEL), _BF16),
            pltpu.VMEM((HALO + tm, tf), _F32),
            pltpu.VMEM((HALO + tm, tf), _F32),
            pltpu.VMEM((N_F, HALO, tf), _F32),
            pltpu.VMEM((N_F, HALO, tf), _F32),
        ],
        compiler_params=pltpu.CompilerParams(
            dimension_semantics=("arbitrary", "arbitrary", "arbitrary"),
            vmem_limit_bytes=VMEM_LIMIT_BYTES),
        name="ffn",
    )(x2, n2g, w_up, w_up, cw, cw, cb, cb, w_down, fg)


def kernel(x, norm1_g, w_in, gm_v_g, gm_v_b, gm_ws, gm_bs, lru_conv_w, lru_conv_b, lru_wa, lru_ba, lru_wx, lru_bx, lru_lambda, gm_out_g, lru_out_g, w_out, norm2_g, ffn_w_up, ffn_conv_w, ffn_conv_b, ffn_w_down, final_g):
    batch, seq, d = x.shape
    assert w_in.shape[0] == 1, "single-layer block: the final norm is fused into the ffn call"
    x2 = x.reshape(batch * seq, d)
    bsf = jnp.repeat(gm_bs[0].T, HEAD_DIM, axis=1)
    wgate = jnp.concatenate([lru_wa[0], lru_wx[0]], axis=-1).astype(_BF16)
    bgate = jnp.concatenate([lru_ba[0], lru_bx[0]], axis=-1)[:, None, :]
    x2 = _mixer(
        x2, norm1_g, w_in[0].astype(_BF16), gm_v_g, gm_v_b, gm_ws[0], bsf,
        lru_conv_w[0], lru_conv_b, wgate, bgate, lru_lambda, gm_out_g, lru_out_g,
        w_out[0].astype(_BF16), batch, seq)
    x2 = _ffn(
        x2, norm2_g, ffn_w_up[0].astype(_BF16), ffn_conv_w[0], ffn_conv_b,
        ffn_w_down[0].astype(_BF16), final_g[None], batch, seq)
    return x2.reshape(batch, seq, d)
```

```python
import jax
import jax.numpy as jnp
from jax import lax
from jax.experimental import pallas as pl
from jax.experimental.pallas import tpu as pltpu

D_MODEL = 2048
GM_WIDTH = 1024
LRU_WIDTH = 1024
CHUNK = 128
HEADS = 8
HEAD_DIM = 128
LRU_CONV = 4
LRU_C = 8.0
D_FF = 6144
FFN_CONV = 3
RMS_EPS = 1e-6
LN_EPS = 1e-5

SUBLANES = 8
HALO = SUBLANES
MIX_VMEM_LIMIT_BYTES = 56 * 1024 * 1024
FFN_VMEM_LIMIT_BYTES = 60 * 1024 * 1024

MIX_TM = 256
FFN_TM = 1024
FFN_TF = 512
N_F = D_FF // FFN_TF

_BF16 = jnp.bfloat16
_F32 = jnp.float32


def _rms(x, g):
    return x * lax.rsqrt(jnp.mean(x * x, axis=-1, keepdims=True) + RMS_EPS) * g


def _dot(a, b):
    return jnp.dot(a, b, preferred_element_type=_F32)


def _mixer_kernel(x_ref, n1g_ref, w_in_ref, vg_ref, vb_ref, ws_ref, bsf_ref,
                  cw_ref, cb_ref, wgate_ref, bgate_ref, lam_ref, gog_ref, log_ref,
                  w_out_ref, o_ref,
                  xl_scr, mix_scr, hcarry_scr):
    tm = MIX_TM
    j = pl.program_id(1)

    @pl.when(j == 0)
    def _():
        xl_scr[0:HALO, :] = jnp.zeros((HALO, LRU_WIDTH), _F32)
        hcarry_scr[...] = jnp.zeros_like(hcarry_scr)

    x = x_ref[...]
    hb = _rms(x, n1g_ref[...]).astype(_BF16)

    z_u = _dot(hb, w_in_ref[:, 0:GM_WIDTH])
    z_v = _dot(hb, w_in_ref[:, GM_WIDTH:2 * GM_WIDTH])
    x_lru = _dot(hb, w_in_ref[:, 2 * GM_WIDTH + LRU_WIDTH:])
    g_lru = _dot(hb, w_in_ref[:, 2 * GM_WIDTH:2 * GM_WIDTH + LRU_WIDTH])

    v = jax.nn.gelu(z_v)
    mu = jnp.mean(v, axis=-1, keepdims=True)
    vc = v - mu
    v = vc * lax.rsqrt(jnp.mean(vc * vc, axis=-1, keepdims=True) + LN_EPS)
    vbf = (v * vg_ref[...] + vb_ref[...]).astype(_BF16)
    tri = (lax.broadcasted_iota(jnp.int32, (CHUNK, CHUNK), 0)
           >= lax.broadcasted_iota(jnp.int32, (CHUNK, CHUNK), 1))
    for h in range(HEADS):
        hs = slice(h * HEAD_DIM, (h + 1) * HEAD_DIM)
        w_h = jnp.where(tri, ws_ref[h], 0.0).astype(_BF16)
        for c in range(0, tm // CHUNK, 2):
            v2 = jnp.concatenate(
                [vbf[c * CHUNK:(c + 1) * CHUNK, hs], vbf[(c + 1) * CHUNK:(c + 2) * CHUNK, hs]],
                axis=1)
            m = _dot(w_h, v2)
            mix_scr[c * CHUNK:(c + 1) * CHUNK, hs] = m[:, :HEAD_DIM] + bsf_ref[:, hs]
            mix_scr[(c + 1) * CHUNK:(c + 2) * CHUNK, hs] = m[:, HEAD_DIM:] + bsf_ref[:, hs]

    xl_scr[HALO:HALO + tm, :] = x_lru
    xr = cb_ref[...] + cw_ref[LRU_CONV - 1:LRU_CONV, :] * x_lru
    for k in range(LRU_CONV - 1):
        off = HALO - (LRU_CONV - 1) + k
        xr = xr + cw_ref[k:k + 1, :] * xl_scr[off:off + tm, :]
    xl_scr[0:HALO, :] = x_lru[tm - HALO:tm, :]
    gz = [_dot(xr[:, h * HEAD_DIM:(h + 1) * HEAD_DIM].astype(_BF16), wgate_ref[h]) + bgate_ref[h]
          for h in range(HEADS)]

    y_gm = _rms(jax.nn.gelu(z_u) * mix_scr[...], gog_ref[...]).astype(_BF16)
    acc = x + _dot(y_gm, w_out_ref[0:GM_WIDTH, :])

    sp = jax.nn.softplus(-lam_ref[...])
    a_parts, b_parts = [], []
    for h in range(HEADS):
        hs = slice(h * HEAD_DIM, (h + 1) * HEAD_DIM)
        r = jax.nn.sigmoid(gz[h][:, :HEAD_DIM])
        i = jax.nn.sigmoid(gz[h][:, HEAD_DIM:])
        log_a = -LRU_C * r * sp[:, hs]
        a_parts.append(jnp.exp(log_a))
        t = jnp.tanh(log_a)
        b_parts.append(jnp.sqrt(-2.0 * t / (1.0 - t)) * (i * xr[:, hs]))
    a_all = jnp.concatenate(a_parts, axis=1)
    b_all = jnp.concatenate(b_parts, axis=1)

    row = lax.broadcasted_iota(jnp.int32, (SUBLANES, LRU_WIDTH), 0)
    h_prev = hcarry_scr[...]
    h_groups = []
    for g in range(tm // SUBLANES):
        a = a_all[g * SUBLANES:(g + 1) * SUBLANES, :]
        b = b_all[g * SUBLANES:(g + 1) * SUBLANES, :]
        for s in (1, 2, 4):
            keep = row >= s
            b = jnp.where(keep, a * pltpu.roll(b, s, 0) + b, b)
            a = jnp.where(keep, a * pltpu.roll(a, s, 0), a)
        hh = a * h_prev + b
        h_groups.append(hh)
        h_prev = hh[SUBLANES - 1:SUBLANES, :]
    hcarry_scr[...] = h_prev
    h_all = jnp.concatenate(h_groups, axis=0)

    y_lru = _rms(h_all * jax.nn.gelu(g_lru), log_ref[...]).astype(_BF16)
    o_ref[...] = acc + _dot(y_lru, w_out_ref[GM_WIDTH:, :])


def _ffn_kernel(x_ref, n2g_ref, wg_ref, wv_ref, cwg_ref, cwv_ref, cbg_ref, cbv_ref,
                wd_ref, fg_ref, o_ref,
                h_scr, ug_scr, uv_scr, carry_g, carry_v):
    tm = FFN_TM
    j = pl.program_id(1)
    f = pl.program_id(2)

    @pl.when(f == 0)
    def _():
        x = x_ref[...]
        h_scr[...] = _rms(x, n2g_ref[...]).astype(_BF16)
        o_ref[...] = x

    @pl.when(j == 0)
    def _():
        carry_g[f] = jnp.zeros((HALO, FFN_TF), _F32)
        carry_v[f] = jnp.zeros((HALO, FFN_TF), _F32)

    hb = h_scr[...]

    def conv(u, scr, carry, cw_ref, cb_ref):
        scr[0:HALO, :] = carry[f]
        scr[HALO:HALO + tm, :] = u
        carry[f] = u[tm - HALO:tm, :]
        y = cb_ref[...] + cw_ref[FFN_CONV - 1:FFN_CONV, :] * u
        for k in range(FFN_CONV - 1):
            off = HALO - (FFN_CONV - 1) + k
            y = y + cw_ref[k:k + 1, :] * scr[off:off + tm, :]
        return y

    gate = conv(_dot(hb, wg_ref[...]), ug_scr, carry_g, cwg_ref, cbg_ref)
    val = conv(_dot(hb, wv_ref[...]), uv_scr, carry_v, cwv_ref, cbv_ref)
    act = (jax.nn.gelu(gate) * val).astype(_BF16)
    o_ref[...] += _dot(act, wd_ref[...])

    @pl.when(f == N_F - 1)
    def _():
        o_ref[...] = _rms(o_ref[...], fg_ref[...])


def _const_spec(shape):
    nd = len(shape)
    return pl.BlockSpec(shape, lambda *_: (0,) * nd, pipeline_mode=pl.Buffered(1))


def _mixer(x2, n1g, w_in, vg, vb, ws, bsf, cw, cb, wgate, bgate, lam, gog, log, w_out,
           batch, seq):
    tm = MIX_TM
    nt = seq // tm
    row_spec = pl.BlockSpec((tm, D_MODEL), lambda b, j: (b * nt + j, 0))
    consts = (n1g, w_in, vg, vb, ws, bsf, cw, cb, wgate, bgate, lam, gog, log, w_out)
    return pl.pallas_call(
        _mixer_kernel,
        grid=(batch, nt),
        in_specs=[row_spec] + [_const_spec(c.shape) for c in consts],
        out_specs=row_spec,
        out_shape=jax.ShapeDtypeStruct(x2.shape, _F32),
        scratch_shapes=[
            pltpu.VMEM((HALO + tm, LRU_WIDTH), _F32),
            pltpu.VMEM((tm, GM_WIDTH), _F32),
            pltpu.VMEM((1, LRU_WIDTH), _F32),
        ],
        compiler_params=pltpu.CompilerParams(
            dimension_semantics=("arbitrary", "arbitrary"),
            vmem_limit_bytes=MIX_VMEM_LIMIT_BYTES),
        name="mixer",
    )(x2, *consts)


def _ffn(x2, n2g, w_up, cw, cb, w_down, fg, batch, seq):
    tm, tf = FFN_TM, FFN_TF
    nt = seq // tm
    x_spec = pl.BlockSpec((tm, D_MODEL), lambda b, j, f: (b * nt + j, 0),
                          pipeline_mode=pl.Buffered(1))
    out_spec = pl.BlockSpec((tm, D_MODEL), lambda b, j, f: (b * nt + j, 0))
    vec_spec = pl.BlockSpec((1, D_MODEL), lambda b, j, f: (0, 0))
    return pl.pallas_call(
        _ffn_kernel,
        grid=(batch, nt, N_F),
        in_specs=[
            x_spec,
            vec_spec,
            pl.BlockSpec((D_MODEL, tf), lambda b, j, f: (0, f)),
            pl.BlockSpec((D_MODEL, tf), lambda b, j, f: (0, N_F + f)),
            pl.BlockSpec((FFN_CONV, tf), lambda b, j, f: (0, f)),
            pl.BlockSpec((FFN_CONV, tf), lambda b, j, f: (0, N_F + f)),
            pl.BlockSpec((1, tf), lambda b, j, f: (0, f)),
            pl.BlockSpec((1, tf), lambda b, j, f: (0, N_F + f)),
            pl.BlockSpec((tf, D_MODEL), lambda b, j, f: (f, 0)),
            vec_spec,
        ],
        out_specs=out_spec,
        out_shape=jax.ShapeDtypeStruct(x2.shape, _F32),
        scratch_shapes=[
            pltpu.VMEM((tm, D_MODEL), _BF16),
            pltpu.VMEM((HALO + tm, tf), _F32),
            pltpu.VMEM((HALO + tm, tf), _F32),
            pltpu.VMEM((N_F, HALO, tf), _F32),
            pltpu.VMEM((N_F, HALO, tf), _F32),
        ],
        compiler_params=pltpu.CompilerParams(
            dimension_semantics=("arbitrary", "arbitrary", "arbitrary"),
            vmem_limit_bytes=FFN_VMEM_LIMIT_BYTES),
        name="ffn",
    )(x2, n2g, w_up, w_up, cw, cw, cb, cb, w_down, fg)


def kernel(x, norm1_g, w_in, gm_v_g, gm_v_b, gm_ws, gm_bs, lru_conv_w, lru_conv_b, lru_wa, lru_ba, lru_wx, lru_bx, lru_lambda, gm_out_g, lru_out_g, w_out, norm2_g, ffn_w_up, ffn_conv_w, ffn_conv_b, ffn_w_down, final_g):
    batch, seq, d = x.shape
    assert w_in.shape[0] == 1, "single-layer block: the final norm is fused into the ffn call"
    x2 = x.reshape(batch * seq, d)
    bsf = jnp.repeat(gm_bs[0].T, HEAD_DIM, axis=1)
    wgate = jnp.concatenate([lru_wa[0], lru_wx[0]], axis=-1).astype(_BF16)
    bgate = jnp.concatenate([lru_ba[0], lru_bx[0]], axis=-1)[:, None, :]
    x2 = _mixer(
        x2, norm1_g, w_in[0].astype(_BF16), gm_v_g, gm_v_b, gm_ws[0], bsf,
        lru_conv_w[0], lru_conv_b, wgate, bgate, lru_lambda, gm_out_g, lru_out_g,
        w_out[0].astype(_BF16), batch, seq)
    x2 = _ffn(
        x2, norm2_g, ffn_w_up[0].astype(_BF16), ffn_conv_w[0], ffn_conv_b,
        ffn_w_down[0].astype(_BF16), final_g[None], batch, seq)
    return x2.reshape(batch, seq, d)
```

```python
import jax
import jax.numpy as jnp
from jax import lax
from jax.experimental import pallas as pl
from jax.experimental.pallas import tpu as pltpu

D_MODEL = 2048
GM_WIDTH = 1024
LRU_WIDTH = 1024
CHUNK = 128
HEADS = 8
HEAD_DIM = 128
LRU_CONV = 4
LRU_C = 8.0
D_FF = 6144
FFN_CONV = 3
RMS_EPS = 1e-6
LN_EPS = 1e-5

SUBLANES = 8
HALO = SUBLANES
MIX_VMEM_LIMIT_BYTES = 60 * 1024 * 1024
FFN_VMEM_LIMIT_BYTES = 60 * 1024 * 1024

MIX_TM = 512
FFN_TM = 1024
FFN_TF = 512
N_F = D_FF // FFN_TF

_BF16 = jnp.bfloat16
_F32 = jnp.float32


def _rms(x, g):
    return x * lax.rsqrt(jnp.mean(x * x, axis=-1, keepdims=True) + RMS_EPS) * g


def _dot(a, b):
    return jnp.dot(a, b, preferred_element_type=_F32)


def _mixer_kernel(x_ref, n1g_ref, w_in_ref, vg_ref, vb_ref, ws_ref, bsf_ref,
                  cw_ref, cb_ref, wgate_ref, bgate_ref, lam_ref, gog_ref, log_ref,
                  w_out_ref, o_ref,
                  xl_scr, mix_scr, hcarry_scr):
    tm = MIX_TM
    j = pl.program_id(1)

    @pl.when(j == 0)
    def _():
        xl_scr[0:HALO, :] = jnp.zeros((HALO, LRU_WIDTH), _F32)
        hcarry_scr[...] = jnp.zeros_like(hcarry_scr)

    x = x_ref[...]
    hb = _rms(x, n1g_ref[...]).astype(_BF16)

    z_u = _dot(hb, w_in_ref[:, 0:GM_WIDTH])
    z_v = _dot(hb, w_in_ref[:, GM_WIDTH:2 * GM_WIDTH])
    x_lru = _dot(hb, w_in_ref[:, 2 * GM_WIDTH + LRU_WIDTH:])
    g_lru = _dot(hb, w_in_ref[:, 2 * GM_WIDTH:2 * GM_WIDTH + LRU_WIDTH])

    v = jax.nn.gelu(z_v)
    mu = jnp.mean(v, axis=-1, keepdims=True)
    vc = v - mu
    v = vc * lax.rsqrt(jnp.mean(vc * vc, axis=-1, keepdims=True) + LN_EPS)
    vbf = (v * vg_ref[...] + vb_ref[...]).astype(_BF16)
    tri = (lax.broadcasted_iota(jnp.int32, (CHUNK, CHUNK), 0)
           >= lax.broadcasted_iota(jnp.int32, (CHUNK, CHUNK), 1))
    for h in range(HEADS):
        hs = slice(h * HEAD_DIM, (h + 1) * HEAD_DIM)
        w_h = jnp.where(tri, ws_ref[h], 0.0).astype(_BF16)
        for c in range(0, tm // CHUNK, 2):
            v2 = jnp.concatenate(
                [vbf[c * CHUNK:(c + 1) * CHUNK, hs], vbf[(c + 1) * CHUNK:(c + 2) * CHUNK, hs]],
                axis=1)
            m = _dot(w_h, v2)
            mix_scr[c * CHUNK:(c + 1) * CHUNK, hs] = m[:, :HEAD_DIM] + bsf_ref[:, hs]
            mix_scr[(c + 1) * CHUNK:(c + 2) * CHUNK, hs] = m[:, HEAD_DIM:] + bsf_ref[:, hs]

    xl_scr[HALO:HALO + tm, :] = x_lru
    xr = cb_ref[...] + cw_ref[LRU_CONV - 1:LRU_CONV, :] * x_lru
    for k in range(LRU_CONV - 1):
        off = HALO - (LRU_CONV - 1) + k
        xr = xr + cw_ref[k:k + 1, :] * xl_scr[off:off + tm, :]
    xl_scr[0:HALO, :] = x_lru[tm - HALO:tm, :]
    gz = [_dot(xr[:, h * HEAD_DIM:(h + 1) * HEAD_DIM].astype(_BF16), wgate_ref[h]) + bgate_ref[h]
          for h in range(HEADS)]

    y_gm = _rms(jax.nn.gelu(z_u) * mix_scr[...], gog_ref[...]).astype(_BF16)
    acc = x + _dot(y_gm, w_out_ref[0:GM_WIDTH, :])

    sp = jax.nn.softplus(-lam_ref[...])
    a_parts, b_parts = [], []
    for h in range(HEADS):
        hs = slice(h * HEAD_DIM, (h + 1) * HEAD_DIM)
        r = jax.nn.sigmoid(gz[h][:, :HEAD_DIM])
        i = jax.nn.sigmoid(gz[h][:, HEAD_DIM:])
        log_a = -LRU_C * r * sp[:, hs]
        a_parts.append(jnp.exp(log_a))
        t = jnp.tanh(log_a)
        b_parts.append(jnp.sqrt(-2.0 * t / (1.0 - t)) * (i * xr[:, hs]))
    a_all = jnp.concatenate(a_parts, axis=1)
    b_all = jnp.concatenate(b_parts, axis=1)

    row = lax.broadcasted_iota(jnp.int32, (SUBLANES, LRU_WIDTH), 0)
    h_prev = hcarry_scr[...]
    h_groups = []
    for g in range(tm // SUBLANES):
        a = a_all[g * SUBLANES:(g + 1) * SUBLANES, :]
        b = b_all[g * SUBLANES:(g + 1) * SUBLANES, :]
        for s in (1, 2, 4):
            keep = row >= s
            b = jnp.where(keep, a * pltpu.roll(b, s, 0) + b, b)
            a = jnp.where(keep, a * pltpu.roll(a, s, 0), a)
        hh = a * h_prev + b
        h_groups.append(hh)
        h_prev = hh[SUBLANES - 1:SUBLANES, :]
    hcarry_scr[...] = h_prev
    h_all = jnp.concatenate(h_groups, axis=0)

    y_lru = _rms(h_all * jax.nn.gelu(g_lru), log_ref[...]).astype(_BF16)
    o_ref[...] = acc + _dot(y_lru, w_out_ref[GM_WIDTH:, :])


def _ffn_kernel(x_ref, n2g_ref, wg_ref, wv_ref, cwg_ref, cwv_ref, cbg_ref, cbv_ref,
                wd_ref, fg_ref, o_ref,
                h_scr, ug_scr, uv_scr, carry_g, carry_v):
    tm = FFN_TM
    j = pl.program_id(1)
    f = pl.program_id(2)

    @pl.when(f == 0)
    def _():
        x = x_ref[...]
        h_scr[...] = _rms(x, n2g_ref[...]).astype(_BF16)
        o_ref[...] = x

    @pl.when(j == 0)
    def _():
        carry_g[f] = jnp.zeros((HALO, FFN_TF), _F32)
        carry_v[f] = jnp.zeros((HALO, FFN_TF), _F32)

    hb = h_scr[...]

    def conv(u, scr, carry, cw_ref, cb_ref):
        scr[0:HALO, :] = carry[f]
        scr[HALO:HALO + tm, :] = u
        carry[f] = u[tm - HALO:tm, :]
        y = cb_ref[...] + cw_ref[FFN_CONV - 1:FFN_CONV, :] * u
        for k in range(FFN_CONV - 1):
            off = HALO - (FFN_CONV - 1) + k
            y = y + cw_ref[k:k + 1, :] * scr[off:off + tm, :]
        return y

    gate = conv(_dot(hb, wg_ref[...]), ug_scr, carry_g, cwg_ref, cbg_ref)
    val = conv(_dot(hb, wv_ref[...]), uv_scr, carry_v, cwv_ref, cbv_ref)
    act = (jax.nn.gelu(gate) * val).astype(_BF16)
    o_ref[...] += _dot(act, wd_ref[...])

    @pl.when(f == N_F - 1)
    def _():
        o_ref[...] = _rms(o_ref[...], fg_ref[...])


def _const_spec(shape):
    nd = len(shape)
    return pl.BlockSpec(shape, lambda *_: (0,) * nd, pipeline_mode=pl.Buffered(1))


def _mixer(x2, n1g, w_in, vg, vb, ws, bsf, cw, cb, wgate, bgate, lam, gog, log, w_out,
           batch, seq):
    tm = MIX_TM
    nt = seq // tm
    row_spec = pl.BlockSpec((tm, D_MODEL), lambda b, j: (b * nt + j, 0))
    consts = (n1g, w_in, vg, vb, ws, bsf, cw, cb, wgate, bgate, lam, gog, log, w_out)
    return pl.pallas_call(
        _mixer_kernel,
        grid=(batch, nt),
        in_specs=[row_spec] + [_const_spec(c.shape) for c in consts],
        out_specs=row_spec,
        out_shape=jax.ShapeDtypeStruct(x2.shape, _F32),
        scratch_shapes=[
            pltpu.VMEM((HALO + tm, LRU_WIDTH), _F32),
            pltpu.VMEM((tm, GM_WIDTH), _F32),
            pltpu.VMEM((1, LRU_WIDTH), _F32),
        ],
        compiler_params=pltpu.CompilerParams(
            dimension_semantics=("arbitrary", "arbitrary"),
            vmem_limit_bytes=MIX_VMEM_LIMIT_BYTES),
        name="mixer",
    )(x2, *consts)


def _ffn(x2, n2g, w_up, cw, cb, w_down, fg, batch, seq):
    tm, tf = FFN_TM, FFN_TF
    nt = seq // tm
    x_spec = pl.BlockSpec((tm, D_MODEL), lambda b, j, f: (b * nt + j, 0),
                          pipeline_mode=pl.Buffered(1))
    out_spec = pl.BlockSpec((tm, D_MODEL), lambda b, j, f: (b * nt + j, 0))
    vec_spec = pl.BlockSpec((1, D_MODEL), lambda b, j, f: (0, 0))
    return pl.pallas_call(
        _ffn_kernel,
        grid=(batch, nt, N_F),
        in_specs=[
            x_spec,
            vec_spec,
            pl.BlockSpec((D_MODEL, tf), lambda b, j, f: (0, f)),
            pl.BlockSpec((D_MODEL, tf), lambda b, j, f: (0, N_F + f)),
            pl.BlockSpec((FFN_CONV, tf), lambda b, j, f: (0, f)),
            pl.BlockSpec((FFN_CONV, tf), lambda b, j, f: (0, N_F + f)),
            pl.BlockSpec((1, tf), lambda b, j, f: (0, f)),
            pl.BlockSpec((1, tf), lambda b, j, f: (0, N_F + f)),
            pl.BlockSpec((tf, D_MODEL), lambda b, j, f: (f, 0)),
            vec_spec,
        ],
        out_specs=out_spec,
        out_shape=jax.ShapeDtypeStruct(x2.shape, _F32),
        scratch_shapes=[
            pltpu.VMEM((tm, D_MODEL), _BF16),
            pltpu.VMEM((HALO + tm, tf), _F32),
            pltpu.VMEM((HALO + tm, tf), _F32),
            pltpu.VMEM((N_F, HALO, tf), _F32),
            pltpu.VMEM((N_F, HALO, tf), _F32),
        ],
        compiler_params=pltpu.CompilerParams(
            dimension_semantics=("arbitrary", "arbitrary", "arbitrary"),
            vmem_limit_bytes=FFN_VMEM_LIMIT_BYTES),
        name="ffn",
    )(x2, n2g, w_up, w_up, cw, cw, cb, cb, w_down, fg)


def kernel(x, norm1_g, w_in, gm_v_g, gm_v_b, gm_ws, gm_bs, lru_conv_w, lru_conv_b, lru_wa, lru_ba, lru_wx, lru_bx, lru_lambda, gm_out_g, lru_out_g, w_out, norm2_g, ffn_w_up, ffn_conv_w, ffn_conv_b, ffn_w_down, final_g):
    batch, seq, d = x.shape
    assert w_in.shape[0] == 1, "single-layer block: the final norm is fused into the ffn call"
    x2 = x.reshape(batch * seq, d)
    bsf = jnp.repeat(gm_bs[0].T, HEAD_DIM, axis=1)
    wgate = jnp.concatenate([lru_wa[0], lru_wx[0]], axis=-1).astype(_BF16)
    bgate = jnp.concatenate([lru_ba[0], lru_bx[0]], axis=-1)[:, None, :]
    x2 = _mixer(
        x2, norm1_g, w_in[0].astype(_BF16), gm_v_g, gm_v_b, gm_ws[0], bsf,
        lru_conv_w[0], lru_conv_b, wgate, bgate, lru_lambda, gm_out_g, lru_out_g,
        w_out[0].astype(_BF16), batch, seq)
    x2 = _ffn(
        x2, norm2_g, ffn_w_up[0].astype(_BF16), ffn_conv_w[0], ffn_conv_b,
        ffn_w_down[0].astype(_BF16), final_g[None], batch, seq)
    return x2.reshape(batch, seq, d)
```

```python
import jax
import jax.numpy as jnp
from jax import lax
from jax.experimental import pallas as pl
from jax.experimental.pallas import tpu as pltpu

D_MODEL = 2048
GM_WIDTH = 1024
LRU_WIDTH = 1024
CHUNK = 128
HEADS = 8
HEAD_DIM = 128
LRU_CONV = 4
LRU_C = 8.0
D_FF = 6144
FFN_CONV = 3
RMS_EPS = 1e-6
LN_EPS = 1e-5

SUBLANES = 8
HALO = SUBLANES
MIX_VMEM_LIMIT_BYTES = 60 * 1024 * 1024
FFN_VMEM_LIMIT_BYTES = 60 * 1024 * 1024

MIX_TM = 512
FFN_TM = 1024
FFN_TF = 512
N_F = D_FF // FFN_TF

_BF16 = jnp.bfloat16
_F32 = jnp.float32


def _rms(x, g):
    return x * lax.rsqrt(jnp.mean(x * x, axis=-1, keepdims=True) + RMS_EPS) * g


def _dot(a, b):
    return jnp.dot(a, b, preferred_element_type=_F32)


def _mixer_kernel(x_ref, n1g_ref, w_in_ref, vg_ref, vb_ref, ws_ref, bsf_ref,
                  cw_ref, cb_ref, wgate_ref, bgate_ref, lam_ref, gog_ref, log_ref,
                  w_out_ref, o_ref,
                  xl_scr, mix_scr, hcarry_scr):
    tm = MIX_TM
    j = pl.program_id(1)

    @pl.when(j == 0)
    def _():
        xl_scr[0:HALO, :] = jnp.zeros((HALO, LRU_WIDTH), _F32)
        hcarry_scr[...] = jnp.zeros_like(hcarry_scr)

    x = x_ref[...]
    hb = _rms(x, n1g_ref[...]).astype(_BF16)

    z_u = _dot(hb, w_in_ref[:, 0:GM_WIDTH])
    z_v = _dot(hb, w_in_ref[:, GM_WIDTH:2 * GM_WIDTH])
    x_lru = _dot(hb, w_in_ref[:, 2 * GM_WIDTH + LRU_WIDTH:])
    g_lru = _dot(hb, w_in_ref[:, 2 * GM_WIDTH:2 * GM_WIDTH + LRU_WIDTH])

    v = jax.nn.gelu(z_v)
    mu = jnp.mean(v, axis=-1, keepdims=True)
    vc = v - mu
    v = vc * lax.rsqrt(jnp.mean(vc * vc, axis=-1, keepdims=True) + LN_EPS)
    vbf = (v * vg_ref[...] + vb_ref[...]).astype(_BF16)
    tri = (lax.broadcasted_iota(jnp.int32, (CHUNK, CHUNK), 0)
           >= lax.broadcasted_iota(jnp.int32, (CHUNK, CHUNK), 1))
    for h in range(HEADS):
        hs = slice(h * HEAD_DIM, (h + 1) * HEAD_DIM)
        w_h = jnp.where(tri, ws_ref[h], 0.0).astype(_BF16)
        for c in range(0, tm // CHUNK, 2):
            v2 = jnp.concatenate(
                [vbf[c * CHUNK:(c + 1) * CHUNK, hs], vbf[(c + 1) * CHUNK:(c + 2) * CHUNK, hs]],
                axis=1)
            m = _dot(w_h, v2)
            mix_scr[c * CHUNK:(c + 1) * CHUNK, hs] = m[:, :HEAD_DIM] + bsf_ref[:, hs]
            mix_scr[(c + 1) * CHUNK:(c + 2) * CHUNK, hs] = m[:, HEAD_DIM:] + bsf_ref[:, hs]

    xl_scr[HALO:HALO + tm, :] = x_lru
    xr = cb_ref[...] + cw_ref[LRU_CONV - 1:LRU_CONV, :] * x_lru
    for k in range(LRU_CONV - 1):
        off = HALO - (LRU_CONV - 1) + k
        xr = xr + cw_ref[k:k + 1, :] * xl_scr[off:off + tm, :]
    xl_scr[0:HALO, :] = x_lru[tm - HALO:tm, :]
    gz = [_dot(xr[:, h * HEAD_DIM:(h + 1) * HEAD_DIM].astype(_BF16), wgate_ref[h]) + bgate_ref[h]
          for h in range(HEADS)]

    y_gm = _rms(jax.nn.gelu(z_u) * mix_scr[...], gog_ref[...]).astype(_BF16)
    acc = x + _dot(y_gm, w_out_ref[0:GM_WIDTH, :])

    sp = jax.nn.softplus(-lam_ref[...])
    a_parts, b_parts = [], []
    for h in range(HEADS):
        hs = slice(h * HEAD_DIM, (h + 1) * HEAD_DIM)
        r = jax.nn.sigmoid(gz[h][:, :HEAD_DIM])
        i = jax.nn.sigmoid(gz[h][:, HEAD_DIM:])
        log_a = -LRU_C * r * sp[:, hs]
        a_parts.append(jnp.exp(log_a))
        t = jnp.tanh(log_a)
        b_parts.append(jnp.sqrt(-2.0 * t / (1.0 - t)) * (i * xr[:, hs]))
    a_all = jnp.concatenate(a_parts, axis=1)
    b_all = jnp.concatenate(b_parts, axis=1)

    row = lax.broadcasted_iota(jnp.int32, (SUBLANES, LRU_WIDTH), 0)
    h_prev = hcarry_scr[...]
    h_groups = []
    for g in range(tm // SUBLANES):
        a = a_all[g * SUBLANES:(g + 1) * SUBLANES, :]
        b = b_all[g * SUBLANES:(g + 1) * SUBLANES, :]
        for s in (1, 2, 4):
            keep = row >= s
            b = jnp.where(keep, a * pltpu.roll(b, s, 0) + b, b)
            a = jnp.where(keep, a * pltpu.roll(a, s, 0), a)
        hh = a * h_prev + b
        h_groups.append(hh)
        h_prev = hh[SUBLANES - 1:SUBLANES, :]
    hcarry_scr[...] = h_prev
    h_all = jnp.concatenate(h_groups, axis=0)

    y_lru = _rms(h_all * jax.nn.gelu(g_lru), log_ref[...]).astype(_BF16)
    o_ref[...] = acc + _dot(y_lru, w_out_ref[GM_WIDTH:, :])


def _ffn_kernel(x_ref, n2g_ref, wg_ref, wv_ref, cwg_ref, cwv_ref, cbg_ref, cbv_ref,
                wd_ref, fg_ref, o_ref,
                h_scr, ug_scr, uv_scr, carry_g, carry_v):
    tm = FFN_TM
    j = pl.program_id(1)
    f = pl.program_id(2)

    @pl.when(f == 0)
    def _():
        x = x_ref[...]
        h_scr[...] = _rms(x, n2g_ref[...]).astype(_BF16)
        o_ref[...] = x

    @pl.when(j == 0)
    def _():
        carry_g[f] = jnp.zeros((HALO, FFN_TF), _F32)
        carry_v[f] = jnp.zeros((HALO, FFN_TF), _F32)

    hb = h_scr[...]

    def conv(u, scr, carry, cw_ref, cb_ref):
        scr[0:HALO, :] = carry[f]
        scr[HALO:HALO + tm, :] = u
        carry[f] = u[tm - HALO:tm, :]
        y = cb_ref[...] + cw_ref[FFN_CONV - 1:FFN_CONV, :] * u
        for k in range(FFN_CONV - 1):
            off = HALO - (FFN_CONV - 1) + k
            y = y + cw_ref[k:k + 1, :] * scr[off:off + tm, :]
        return y

    gate = conv(_dot(hb, wg_ref[...]), ug_scr, carry_g, cwg_ref, cbg_ref)
    val = conv(_dot(hb, wv_ref[...]), uv_scr, carry_v, cwv_ref, cbv_ref)
    act = (jax.nn.gelu(gate) * val).astype(_BF16)
    o_ref[...] += _dot(act, wd_ref[...])

    @pl.when(f == N_F - 1)
    def _():
        o_ref[...] = _rms(o_ref[...], fg_ref[...])


def _const_spec(shape):
    nd = len(shape)
    return pl.BlockSpec(shape, lambda *_: (0,) * nd, pipeline_mode=pl.Buffered(1))


def _mixer(x2, n1g, w_in, vg, vb, ws, bsf, cw, cb, wgate, bgate, lam, gog, log, w_out,
           batch, seq):
    tm = MIX_TM
    nt = seq // tm
    row_spec = pl.BlockSpec((tm, D_MODEL), lambda b, j: (b * nt + j, 0))
    consts = (n1g, w_in, vg, vb, ws, bsf, cw, cb, wgate, bgate, lam, gog, log, w_out)
    return pl.pallas_call(
        _mixer_kernel,
        grid=(batch, nt),
        in_specs=[row_spec] + [_const_spec(c.shape) for c in consts],
        out_specs=row_spec,
        out_shape=jax.ShapeDtypeStruct(x2.shape, _F32),
        scratch_shapes=[
            pltpu.VMEM((HALO + tm, LRU_WIDTH), _F32),
            pltpu.VMEM((tm, GM_WIDTH), _F32),
            pltpu.VMEM((1, LRU_WIDTH), _F32),
        ],
        compiler_params=pltpu.CompilerParams(
            dimension_semantics=("arbitrary", "arbitrary"),
            vmem_limit_bytes=MIX_VMEM_LIMIT_BYTES),
        name="mixer",
    )(x2, *consts)


def _ffn(x2, n2g, w_up, cw, cb, w_down, fg, batch, seq):
    tm, tf = FFN_TM, FFN_TF
    nt = seq // tm
    x_spec = pl.BlockSpec((tm, D_MODEL), lambda b, j, f: (b * nt + j, 0))
    out_spec = x_spec
    vec_spec = pl.BlockSpec((1, D_MODEL), lambda b, j, f: (0, 0))
    return pl.pallas_call(
        _ffn_kernel,
        grid=(batch, nt, N_F),
        in_specs=[
            x_spec,
            vec_spec,
            pl.BlockSpec((D_MODEL, tf), lambda b, j, f: (0, f)),
            pl.BlockSpec((D_MODEL, tf), lambda b, j, f: (0, N_F + f)),
            pl.BlockSpec((FFN_CONV, tf), lambda b, j, f: (0, f)),
            pl.BlockSpec((FFN_CONV, tf), lambda b, j, f: (0, N_F + f)),
            pl.BlockSpec((1, tf), lambda b, j, f: (0, f)),
            pl.BlockSpec((1, tf), lambda b, j, f: (0, N_F + f)),
            pl.BlockSpec((tf, D_MODEL), lambda b, j, f: (f, 0)),
            vec_spec,
        ],
        out_specs=out_spec,
        out_shape=jax.ShapeDtypeStruct(x2.shape, _F32),
        scratch_shapes=[
            pltpu.VMEM((tm, D_MODEL), _BF16),
            pltpu.VMEM((HALO + tm, tf), _F32),
            pltpu.VMEM((HALO + tm, tf), _F32),
            pltpu.VMEM((N_F, HALO, tf), _F32),
            pltpu.VMEM((N_F, HALO, tf), _F32),
        ],
        compiler_params=pltpu.CompilerParams(
            dimension_semantics=("arbitrary", "arbitrary", "arbitrary"),
            vmem_limit_bytes=FFN_VMEM_LIMIT_BYTES),
        name="ffn",
    )(x2, n2g, w_up, w_up, cw, cw, cb, cb, w_down, fg)


def kernel(x, norm1_g, w_in, gm_v_g, gm_v_b, gm_ws, gm_bs, lru_conv_w, lru_conv_b, lru_wa, lru_ba, lru_wx, lru_bx, lru_lambda, gm_out_g, lru_out_g, w_out, norm2_g, ffn_w_up, ffn_conv_w, ffn_conv_b, ffn_w_down, final_g):
    batch, seq, d = x.shape
    assert w_in.shape[0] == 1, "single-layer block: the final norm is fused into the ffn call"
    x2 = x.reshape(batch * seq, d)
    bsf = jnp.repeat(gm_bs[0].T, HEAD_DIM, axis=1)
    wgate = jnp.concatenate([lru_wa[0], lru_wx[0]], axis=-1).astype(_BF16)
    bgate = jnp.concatenate([lru_ba[0], lru_bx[0]], axis=-1)[:, None, :]
    x2 = _mixer(
        x2, norm1_g, w_in[0].astype(_BF16), gm_v_g, gm_v_b, gm_ws[0], bsf,
        lru_conv_w[0], lru_conv_b, wgate, bgate, lru_lambda, gm_out_g, lru_out_g,
        w_out[0].astype(_BF16), batch, seq)
    x2 = _ffn(
        x2, norm2_g, ffn_w_up[0].astype(_BF16), ffn_conv_w[0], ffn_conv_b,
        ffn_w_down[0].astype(_BF16), final_g[None], batch, seq)
    return x2.reshape(batch, seq, d)
```

```python
import functools

import jax
import jax.numpy as jnp
from jax import lax
from jax.experimental import pallas as pl
from jax.experimental.pallas import tpu as pltpu

D_MODEL = 2048
GM_WIDTH = 1024
LRU_WIDTH = 1024
CHUNK = 128
HEADS = 8
HEAD_DIM = 128
LRU_CONV = 4
LRU_C = 8.0
D_FF = 6144
FFN_CONV = 3
RMS_EPS = 1e-6
LN_EPS = 1e-5

SUBLANES = 8
HALO = SUBLANES
MIX_VMEM_LIMIT_BYTES = 60 * 1024 * 1024
FFN_VMEM_LIMIT_BYTES = 62 * 1024 * 1024

MIX_TM = 512
FFN_TM = 1024
FFN_TF = 768
N_F = D_FF // FFN_TF

_BF16 = jnp.bfloat16
_F32 = jnp.float32


def _rms(x, g):
    return x * lax.rsqrt(jnp.mean(x * x, axis=-1, keepdims=True) + RMS_EPS) * g


def _dot(a, b):
    return jnp.dot(a, b, preferred_element_type=_F32)


def _mixer_kernel(x_ref, n1g_ref, w_in_ref, vg_ref, vb_ref, ws_ref, bsf_ref,
                  cw_ref, cb_ref, wgate_ref, bgate_ref, lam_ref, gog_ref, log_ref,
                  w_out_ref, o_ref,
                  xl_scr, mix_scr, hcarry_scr):
    tm = MIX_TM
    j = pl.program_id(1)

    @pl.when(j == 0)
    def _():
        xl_scr[0:HALO, :] = jnp.zeros((HALO, LRU_WIDTH), _F32)
        hcarry_scr[...] = jnp.zeros_like(hcarry_scr)

    x = x_ref[...]
    hb = _rms(x, n1g_ref[...]).astype(_BF16)

    z_u = _dot(hb, w_in_ref[:, 0:GM_WIDTH])
    z_v = _dot(hb, w_in_ref[:, GM_WIDTH:2 * GM_WIDTH])
    x_lru = _dot(hb, w_in_ref[:, 2 * GM_WIDTH + LRU_WIDTH:])
    g_lru = _dot(hb, w_in_ref[:, 2 * GM_WIDTH:2 * GM_WIDTH + LRU_WIDTH])

    v = jax.nn.gelu(z_v)
    mu = jnp.mean(v, axis=-1, keepdims=True)
    vc = v - mu
    v = vc * lax.rsqrt(jnp.mean(vc * vc, axis=-1, keepdims=True) + LN_EPS)
    vbf = (v * vg_ref[...] + vb_ref[...]).astype(_BF16)
    tri = (lax.broadcasted_iota(jnp.int32, (CHUNK, CHUNK), 0)
           >= lax.broadcasted_iota(jnp.int32, (CHUNK, CHUNK), 1))
    for h in range(HEADS):
        hs = slice(h * HEAD_DIM, (h + 1) * HEAD_DIM)
        w_h = jnp.where(tri, ws_ref[h], 0.0).astype(_BF16)
        for c in range(0, tm // CHUNK, 2):
            v2 = jnp.concatenate(
                [vbf[c * CHUNK:(c + 1) * CHUNK, hs], vbf[(c + 1) * CHUNK:(c + 2) * CHUNK, hs]],
                axis=1)
            m = _dot(w_h, v2)
            mix_scr[c * CHUNK:(c + 1) * CHUNK, hs] = m[:, :HEAD_DIM] + bsf_ref[:, hs]
            mix_scr[(c + 1) * CHUNK:(c + 2) * CHUNK, hs] = m[:, HEAD_DIM:] + bsf_ref[:, hs]

    xl_scr[HALO:HALO + tm, :] = x_lru
    xr = cb_ref[...] + cw_ref[LRU_CONV - 1:LRU_CONV, :] * x_lru
    for k in range(LRU_CONV - 1):
        off = HALO - (LRU_CONV - 1) + k
        xr = xr + cw_ref[k:k + 1, :] * xl_scr[off:off + tm, :]
    xl_scr[0:HALO, :] = x_lru[tm - HALO:tm, :]
    gz = [_dot(xr[:, h * HEAD_DIM:(h + 1) * HEAD_DIM].astype(_BF16), wgate_ref[h]) + bgate_ref[h]
          for h in range(HEADS)]

    y_gm = _rms(jax.nn.gelu(z_u) * mix_scr[...], gog_ref[...]).astype(_BF16)
    acc = x + _dot(y_gm, w_out_ref[0:GM_WIDTH, :])

    sp = jax.nn.softplus(-lam_ref[...])
    a_parts, b_parts = [], []
    for h in range(HEADS):
        hs = slice(h * HEAD_DIM, (h + 1) * HEAD_DIM)
        r = jax.nn.sigmoid(gz[h][:, :HEAD_DIM])
        i = jax.nn.sigmoid(gz[h][:, HEAD_DIM:])
        log_a = -LRU_C * r * sp[:, hs]
        a_parts.append(jnp.exp(log_a))
        t = jnp.tanh(log_a)
        b_parts.append(jnp.sqrt(-2.0 * t / (1.0 - t)) * (i * xr[:, hs]))
    a_all = jnp.concatenate(a_parts, axis=1)
    b_all = jnp.concatenate(b_parts, axis=1)

    row = lax.broadcasted_iota(jnp.int32, (SUBLANES, LRU_WIDTH), 0)
    h_prev = hcarry_scr[...]
    h_groups = []
    for g in range(tm // SUBLANES):
        a = a_all[g * SUBLANES:(g + 1) * SUBLANES, :]
        b = b_all[g * SUBLANES:(g + 1) * SUBLANES, :]
        for s in (1, 2, 4):
            keep = row >= s
            b = jnp.where(keep, a * pltpu.roll(b, s, 0) + b, b)
            a = jnp.where(keep, a * pltpu.roll(a, s, 0), a)
        hh = a * h_prev + b
        h_groups.append(hh)
        h_prev = hh[SUBLANES - 1:SUBLANES, :]
    hcarry_scr[...] = h_prev
    h_all = jnp.concatenate(h_groups, axis=0)

    y_lru = _rms(h_all * jax.nn.gelu(g_lru), log_ref[...]).astype(_BF16)
    o_ref[...] = acc + _dot(y_lru, w_out_ref[GM_WIDTH:, :])


def _ffn_kernel(n_tiles, x_hbm, n2g_ref, wg_ref, wv_ref, cwg_ref, cwv_ref, cbg_ref, cbv_ref,
                wd_ref, fg_ref, o_ref,
                x_scr, x_sem, h_scr, ug_scr, uv_scr, carry_g, carry_v):
    tm = FFN_TM
    j = pl.program_id(1)
    f = pl.program_id(2)
    tile = pl.program_id(0) * pl.num_programs(1) + j

    def x_copy(t):
        return pltpu.make_async_copy(x_hbm.at[pl.ds(t * tm, tm), :], x_scr, x_sem)

    @pl.when(f == 0)
    def _():
        @pl.when(tile == 0)
        def _():
            x_copy(tile).start()

        x_copy(tile).wait()
        x = x_scr[...]
        h_scr[...] = _rms(x, n2g_ref[...]).astype(_BF16)
        o_ref[...] = x

    @pl.when(jnp.logical_and(f == 1, tile + 1 < n_tiles))
    def _():
        x_copy(tile + 1).start()

    @pl.when(j == 0)
    def _():
        carry_g[f] = jnp.zeros((HALO, FFN_TF), _F32)
        carry_v[f] = jnp.zeros((HALO, FFN_TF), _F32)

    hb = h_scr[...]

    def conv(u, scr, carry, cw_ref, cb_ref):
        scr[0:HALO, :] = carry[f]
        scr[HALO:HALO + tm, :] = u
        carry[f] = u[tm - HALO:tm, :]
        y = cb_ref[...] + cw_ref[FFN_CONV - 1:FFN_CONV, :] * u
        for k in range(FFN_CONV - 1):
            off = HALO - (FFN_CONV - 1) + k
            y = y + cw_ref[k:k + 1, :] * scr[off:off + tm, :]
        return y

    gate = conv(_dot(hb, wg_ref[...]), ug_scr, carry_g, cwg_ref, cbg_ref)
    val = conv(_dot(hb, wv_ref[...]), uv_scr, carry_v, cwv_ref, cbv_ref)
    act = (jax.nn.gelu(gate) * val).astype(_BF16)
    o_ref[...] += _dot(act, wd_ref[...])

    @pl.when(f == N_F - 1)
    def _():
        o_ref[...] = _rms(o_ref[...], fg_ref[...])


def _const_spec(shape):
    nd = len(shape)
    return pl.BlockSpec(shape, lambda *_: (0,) * nd, pipeline_mode=pl.Buffered(1))


def _mixer(x2, n1g, w_in, vg, vb, ws, bsf, cw, cb, wgate, bgate, lam, gog, log, w_out,
           batch, seq):
    tm = MIX_TM
    nt = seq // tm
    row_spec = pl.BlockSpec((tm, D_MODEL), lambda b, j: (b * nt + j, 0))
    consts = (n1g, w_in, vg, vb, ws, bsf, cw, cb, wgate, bgate, lam, gog, log, w_out)
    return pl.pallas_call(
        _mixer_kernel,
        grid=(batch, nt),
        in_specs=[row_spec] + [_const_spec(c.shape) for c in consts],
        out_specs=row_spec,
        out_shape=jax.ShapeDtypeStruct(x2.shape, _F32),
        scratch_shapes=[
            pltpu.VMEM((HALO + tm, LRU_WIDTH), _F32),
            pltpu.VMEM((tm, GM_WIDTH), _F32),
            pltpu.VMEM((1, LRU_WIDTH), _F32),
        ],
        compiler_params=pltpu.CompilerParams(
            dimension_semantics=("arbitrary", "arbitrary"),
            vmem_limit_bytes=MIX_VMEM_LIMIT_BYTES),
        name="mixer",
    )(x2, *consts)


def _ffn(x2, n2g, w_up, cw, cb, w_down, fg, batch, seq):
    tm, tf = FFN_TM, FFN_TF
    nt = seq // tm
    out_spec = pl.BlockSpec((tm, D_MODEL), lambda b, j, f: (b * nt + j, 0))
    vec_spec = pl.BlockSpec((1, D_MODEL), lambda b, j, f: (0, 0))
    return pl.pallas_call(
        functools.partial(_ffn_kernel, batch * nt),
        grid=(batch, nt, N_F),
        in_specs=[
            pl.BlockSpec(memory_space=pl.ANY),
            vec_spec,
            pl.BlockSpec((D_MODEL, tf), lambda b, j, f: (0, f)),
            pl.BlockSpec((D_MODEL, tf), lambda b, j, f: (0, N_F + f)),
            pl.BlockSpec((FFN_CONV, tf), lambda b, j, f: (0, f)),
            pl.BlockSpec((FFN_CONV, tf), lambda b, j, f: (0, N_F + f)),
            pl.BlockSpec((1, tf), lambda b, j, f: (0, f)),
            pl.BlockSpec((1, tf), lambda b, j, f: (0, N_F + f)),
            pl.BlockSpec((tf, D_MODEL), lambda b, j, f: (f, 0)),
            vec_spec,
        ],
        out_specs=out_spec,
        out_shape=jax.ShapeDtypeStruct(x2.shape, _F32),
        scratch_shapes=[
            pltpu.VMEM((tm, D_MODEL), _F32),
            pltpu.SemaphoreType.DMA(()),
            pltpu.VMEM((tm, D_MODEL), _BF16),
            pltpu.VMEM((HALO + tm, tf), _F32),
            pltpu.VMEM((HALO + tm, tf), _F32),
            pltpu.VMEM((N_F, HALO, tf), _F32),
            pltpu.VMEM((N_F, HALO, tf), _F32),
        ],
        compiler_params=pltpu.CompilerParams(
            dimension_semantics=("arbitrary", "arbitrary", "arbitrary"),
            vmem_limit_bytes=FFN_VMEM_LIMIT_BYTES),
        name="ffn",
    )(x2, n2g, w_up, w_up, cw, cw, cb, cb, w_down, fg)


def kernel(x, norm1_g, w_in, gm_v_g, gm_v_b, gm_ws, gm_bs, lru_conv_w, lru_conv_b, lru_wa, lru_ba, lru_wx, lru_bx, lru_lambda, gm_out_g, lru_out_g, w_out, norm2_g, ffn_w_up, ffn_conv_w, ffn_conv_b, ffn_w_down, final_g):
    batch, seq, d = x.shape
    assert w_in.shape[0] == 1, "single-layer block: the final norm is fused into the ffn call"
    x2 = x.reshape(batch * seq, d)
    bsf = jnp.repeat(gm_bs[0].T, HEAD_DIM, axis=1)
    wgate = jnp.concatenate([lru_wa[0], lru_wx[0]], axis=-1).astype(_BF16)
    bgate = jnp.concatenate([lru_ba[0], lru_bx[0]], axis=-1)[:, None, :]
    x2 = _mixer(
        x2, norm1_g, w_in[0].astype(_BF16), gm_v_g, gm_v_b, gm_ws[0], bsf,
        lru_conv_w[0], lru_conv_b, wgate, bgate, lru_lambda, gm_out_g, lru_out_g,
        w_out[0].astype(_BF16), batch, seq)
    x2 = _ffn(
        x2, norm2_g, ffn_w_up[0].astype(_BF16), ffn_conv_w[0], ffn_conv_b,
        ffn_w_down[0].astype(_BF16), final_g[None], batch, seq)
    return x2.reshape(batch, seq, d)
```

```python
import functools

import jax
import jax.numpy as jnp
from jax import lax
from jax.experimental import pallas as pl
from jax.experimental.pallas import tpu as pltpu

D_MODEL = 2048
GM_WIDTH = 1024
LRU_WIDTH = 1024
CHUNK = 128
HEADS = 8
HEAD_DIM = 128
LRU_CONV = 4
LRU_C = 8.0
D_FF = 6144
FFN_CONV = 3
RMS_EPS = 1e-6
LN_EPS = 1e-5

SUBLANES = 8
HALO = SUBLANES
MIX_VMEM_LIMIT_BYTES = 60 * 1024 * 1024
FFN_VMEM_LIMIT_BYTES = 62 * 1024 * 1024

MIX_TM = 512
FFN_TM = 1024
FFN_TF = 768
N_F = D_FF // FFN_TF

_BF16 = jnp.bfloat16
_F32 = jnp.float32


def _rms(x, g):
    return x * lax.rsqrt(jnp.mean(x * x, axis=-1, keepdims=True) + RMS_EPS) * g


def _dot(a, b):
    return jnp.dot(a, b, preferred_element_type=_F32)


def _mixer_kernel(x_ref, n1g_ref, w_in_ref, vg_ref, vb_ref, ws_ref, bsf_ref,
                  cw_ref, cb_ref, wgate_ref, bgate_ref, lam_ref, gog_ref, log_ref,
                  w_out_ref, o_ref,
                  xl_scr, mix_scr, hcarry_scr):
    tm = MIX_TM
    j = pl.program_id(1)

    @pl.when(j == 0)
    def _():
        xl_scr[0:HALO, :] = jnp.zeros((HALO, LRU_WIDTH), _F32)
        hcarry_scr[...] = jnp.zeros_like(hcarry_scr)

    x = x_ref[...]
    hb = _rms(x, n1g_ref[...]).astype(_BF16)

    x_lru = _dot(hb, w_in_ref[:, 2 * GM_WIDTH + LRU_WIDTH:])

    xl_scr[HALO:HALO + tm, :] = x_lru
    xr = cb_ref[...] + cw_ref[LRU_CONV - 1:LRU_CONV, :] * x_lru
    for k in range(LRU_CONV - 1):
        off = HALO - (LRU_CONV - 1) + k
        xr = xr + cw_ref[k:k + 1, :] * xl_scr[off:off + tm, :]
    xl_scr[0:HALO, :] = x_lru[tm - HALO:tm, :]
    z_v = _dot(hb, w_in_ref[:, GM_WIDTH:2 * GM_WIDTH])
    gz = [_dot(xr[:, h * HEAD_DIM:(h + 1) * HEAD_DIM].astype(_BF16), wgate_ref[h]) + bgate_ref[h]
          for h in range(HEADS)]

    v = jax.nn.gelu(z_v)
    mu = jnp.mean(v, axis=-1, keepdims=True)
    vc = v - mu
    v = vc * lax.rsqrt(jnp.mean(vc * vc, axis=-1, keepdims=True) + LN_EPS)
    vbf = (v * vg_ref[...] + vb_ref[...]).astype(_BF16)
    tri = (lax.broadcasted_iota(jnp.int32, (CHUNK, CHUNK), 0)
           >= lax.broadcasted_iota(jnp.int32, (CHUNK, CHUNK), 1))
    z_u = _dot(hb, w_in_ref[:, 0:GM_WIDTH])
    for h in range(HEADS):
        hs = slice(h * HEAD_DIM, (h + 1) * HEAD_DIM)
        w_h = jnp.where(tri, ws_ref[h], 0.0).astype(_BF16)
        for c in range(0, tm // CHUNK, 2):
            v2 = jnp.concatenate(
                [vbf[c * CHUNK:(c + 1) * CHUNK, hs], vbf[(c + 1) * CHUNK:(c + 2) * CHUNK, hs]],
                axis=1)
            m = _dot(w_h, v2)
            mix_scr[c * CHUNK:(c + 1) * CHUNK, hs] = m[:, :HEAD_DIM] + bsf_ref[:, hs]
            mix_scr[(c + 1) * CHUNK:(c + 2) * CHUNK, hs] = m[:, HEAD_DIM:] + bsf_ref[:, hs]
    g_lru = _dot(hb, w_in_ref[:, 2 * GM_WIDTH:2 * GM_WIDTH + LRU_WIDTH])

    y_gm = _rms(jax.nn.gelu(z_u) * mix_scr[...], gog_ref[...]).astype(_BF16)
    acc = x + _dot(y_gm, w_out_ref[0:GM_WIDTH, :])

    sp = jax.nn.softplus(-lam_ref[...])
    a_parts, b_parts = [], []
    for h in range(HEADS):
        hs = slice(h * HEAD_DIM, (h + 1) * HEAD_DIM)
        r = jax.nn.sigmoid(gz[h][:, :HEAD_DIM])
        i = jax.nn.sigmoid(gz[h][:, HEAD_DIM:])
        log_a = -LRU_C * r * sp[:, hs]
        a_parts.append(jnp.exp(log_a))
        t = jnp.tanh(log_a)
        b_parts.append(jnp.sqrt(-2.0 * t / (1.0 - t)) * (i * xr[:, hs]))
    a_all = jnp.concatenate(a_parts, axis=1)
    b_all = jnp.concatenate(b_parts, axis=1)

    row = lax.broadcasted_iota(jnp.int32, (SUBLANES, LRU_WIDTH), 0)
    h_prev = hcarry_scr[...]
    h_groups = []
    for g in range(tm // SUBLANES):
        a = a_all[g * SUBLANES:(g + 1) * SUBLANES, :]
        b = b_all[g * SUBLANES:(g + 1) * SUBLANES, :]
        for s in (1, 2, 4):
            keep = row >= s
            b = jnp.where(keep, a * pltpu.roll(b, s, 0) + b, b)
            a = jnp.where(keep, a * pltpu.roll(a, s, 0), a)
        hh = a * h_prev + b
        h_groups.append(hh)
        h_prev = hh[SUBLANES - 1:SUBLANES, :]
    hcarry_scr[...] = h_prev
    h_all = jnp.concatenate(h_groups, axis=0)

    y_lru = _rms(h_all * jax.nn.gelu(g_lru), log_ref[...]).astype(_BF16)
    o_ref[...] = acc + _dot(y_lru, w_out_ref[GM_WIDTH:, :])


def _ffn_kernel(n_tiles, x_hbm, n2g_ref, wg_ref, wv_ref, cwg_ref, cwv_ref, cbg_ref, cbv_ref,
                wd_ref, fg_ref, o_ref,
                x_scr, x_sem, h_scr, ug_scr, uv_scr, carry_g, carry_v):
    tm = FFN_TM
    j = pl.program_id(1)
    f = pl.program_id(2)
    tile = pl.program_id(0) * pl.num_programs(1) + j

    def x_copy(t):
        return pltpu.make_async_copy(x_hbm.at[pl.ds(t * tm, tm), :], x_scr, x_sem)

    @pl.when(f == 0)
    def _():
        @pl.when(tile == 0)
        def _():
            x_copy(tile).start()

        x_copy(tile).wait()
        x = x_scr[...]
        h_scr[...] = _rms(x, n2g_ref[...]).astype(_BF16)
        o_ref[...] = x

    @pl.when(jnp.logical_and(f == 1, tile + 1 < n_tiles))
    def _():
        x_copy(tile + 1).start()

    @pl.when(j == 0)
    def _():
        carry_g[f] = jnp.zeros((HALO, FFN_TF), _F32)
        carry_v[f] = jnp.zeros((HALO, FFN_TF), _F32)

    hb = h_scr[...]

    def conv(u, scr, carry, cw_ref, cb_ref):
        scr[0:HALO, :] = carry[f]
        scr[HALO:HALO + tm, :] = u
        carry[f] = u[tm - HALO:tm, :]
        y = cb_ref[...] + cw_ref[FFN_CONV - 1:FFN_CONV, :] * u
        for k in range(FFN_CONV - 1):
            off = HALO - (FFN_CONV - 1) + k
            y = y + cw_ref[k:k + 1, :] * scr[off:off + tm, :]
        return y

    gate = conv(_dot(hb, wg_ref[...]), ug_scr, carry_g, cwg_ref, cbg_ref)
    val = conv(_dot(hb, wv_ref[...]), uv_scr, carry_v, cwv_ref, cbv_ref)
    act = (jax.nn.gelu(gate) * val).astype(_BF16)
    o_ref[...] += _dot(act, wd_ref[...])

    @pl.when(f == N_F - 1)
    def _():
        o_ref[...] = _rms(o_ref[...], fg_ref[...])


def _const_spec(shape):
    nd = len(shape)
    return pl.BlockSpec(shape, lambda *_: (0,) * nd, pipeline_mode=pl.Buffered(1))


def _mixer(x2, n1g, w_in, vg, vb, ws, bsf, cw, cb, wgate, bgate, lam, gog, log, w_out,
           batch, seq):
    tm = MIX_TM
    nt = seq // tm
    row_spec = pl.BlockSpec((tm, D_MODEL), lambda b, j: (b * nt + j, 0))
    consts = (n1g, w_in, vg, vb, ws, bsf, cw, cb, wgate, bgate, lam, gog, log, w_out)
    return pl.pallas_call(
        _mixer_kernel,
        grid=(batch, nt),
        in_specs=[row_spec] + [_const_spec(c.shape) for c in consts],
        out_specs=row_spec,
        out_shape=jax.ShapeDtypeStruct(x2.shape, _F32),
        scratch_shapes=[
            pltpu.VMEM((HALO + tm, LRU_WIDTH), _F32),
            pltpu.VMEM((tm, GM_WIDTH), _F32),
            pltpu.VMEM((1, LRU_WIDTH), _F32),
        ],
        compiler_params=pltpu.CompilerParams(
            dimension_semantics=("arbitrary", "arbitrary"),
            vmem_limit_bytes=MIX_VMEM_LIMIT_BYTES),
        name="mixer",
    )(x2, *consts)


def _ffn(x2, n2g, w_up, cw, cb, w_down, fg, batch, seq):
    tm, tf = FFN_TM, FFN_TF
    nt = seq // tm
    out_spec = pl.BlockSpec((tm, D_MODEL), lambda b, j, f: (b * nt + j, 0))
    vec_spec = pl.BlockSpec((1, D_MODEL), lambda b, j, f: (0, 0))
    return pl.pallas_call(
        functools.partial(_ffn_kernel, batch * nt),
        grid=(batch, nt, N_F),
        in_specs=[
            pl.BlockSpec(memory_space=pl.ANY),
            vec_spec,
            pl.BlockSpec((D_MODEL, tf), lambda b, j, f: (0, f)),
            pl.BlockSpec((D_MODEL, tf), lambda b, j, f: (0, N_F + f)),
            pl.BlockSpec((FFN_CONV, tf), lambda b, j, f: (0, f)),
            pl.BlockSpec((FFN_CONV, tf), lambda b, j, f: (0, N_F + f)),
            pl.BlockSpec((1, tf), lambda b, j, f: (0, f)),
            pl.BlockSpec((1, tf), lambda b, j, f: (0, N_F + f)),
            pl.BlockSpec((tf, D_MODEL), lambda b, j, f: (f, 0)),
            vec_spec,
        ],
        out_specs=out_spec,
        out_shape=jax.ShapeDtypeStruct(x2.shape, _F32),
        scratch_shapes=[
            pltpu.VMEM((tm, D_MODEL), _F32),
            pltpu.SemaphoreType.DMA(()),
            pltpu.VMEM((tm, D_MODEL), _BF16),
            pltpu.VMEM((HALO + tm, tf), _F32),
            pltpu.VMEM((HALO + tm, tf), _F32),
            pltpu.VMEM((N_F, HALO, tf), _F32),
            pltpu.VMEM((N_F, HALO, tf), _F32),
        ],
        compiler_params=pltpu.CompilerParams(
            dimension_semantics=("arbitrary", "arbitrary", "arbitrary"),
            vmem_limit_bytes=FFN_VMEM_LIMIT_BYTES),
        name="ffn",
    )(x2, n2g, w_up, w_up, cw, cw, cb, cb, w_down, fg)


def kernel(x, norm1_g, w_in, gm_v_g, gm_v_b, gm_ws, gm_bs, lru_conv_w, lru_conv_b, lru_wa, lru_ba, lru_wx, lru_bx, lru_lambda, gm_out_g, lru_out_g, w_out, norm2_g, ffn_w_up, ffn_conv_w, ffn_conv_b, ffn_w_down, final_g):
    batch, seq, d = x.shape
    assert w_in.shape[0] == 1, "single-layer block: the final norm is fused into the ffn call"
    x2 = x.reshape(batch * seq, d)
    bsf = jnp.repeat(gm_bs[0].T, HEAD_DIM, axis=1)
    wgate = jnp.concatenate([lru_wa[0], lru_wx[0]], axis=-1).astype(_BF16)
    bgate = jnp.concatenate([lru_ba[0], lru_bx[0]], axis=-1)[:, None, :]
    x2 = _mixer(
        x2, norm1_g, w_in[0].astype(_BF16), gm_v_g, gm_v_b, gm_ws[0], bsf,
        lru_conv_w[0], lru_conv_b, wgate, bgate, lru_lambda, gm_out_g, lru_out_g,
        w_out[0].astype(_BF16), batch, seq)
    x2 = _ffn(
        x2, norm2_g, ffn_w_up[0].astype(_BF16), ffn_conv_w[0], ffn_conv_b,
        ffn_w_down[0].astype(_BF16), final_g[None], batch, seq)
    return x2.reshape(batch, seq, d)
```

```python
import functools

import jax
import jax.numpy as jnp
from jax import lax
from jax.experimental import pallas as pl
from jax.experimental.pallas import tpu as pltpu

D_MODEL = 2048
GM_WIDTH = 1024
LRU_WIDTH = 1024
CHUNK = 128
HEADS = 8
HEAD_DIM = 128
LRU_CONV = 4
LRU_C = 8.0
D_FF = 6144
FFN_CONV = 3
RMS_EPS = 1e-6
LN_EPS = 1e-5

SUBLANES = 8
HALO = SUBLANES
MIX_VMEM_LIMIT_BYTES = 60 * 1024 * 1024
FFN_VMEM_LIMIT_BYTES = 62 * 1024 * 1024

MIX_TM = 512
FFN_TM = 1024
FFN_TF = 768
N_F = D_FF // FFN_TF

_BF16 = jnp.bfloat16
_F32 = jnp.float32


def _rms(x, g):
    return x * lax.rsqrt(jnp.mean(x * x, axis=-1, keepdims=True) + RMS_EPS) * g


def _dot(a, b):
    return jnp.dot(a, b, preferred_element_type=_F32)


def _mixer_kernel(x_ref, n1g_ref, w_in_ref, vg_ref, vb_ref, ws_ref, bsf_ref,
                  cw_ref, cb_ref, wgate_ref, bgate_ref, lam_ref, gog_ref, log_ref,
                  w_out_ref, o_ref,
                  xl_scr, mix_scr, hcarry_scr):
    tm = MIX_TM
    j = pl.program_id(1)

    @pl.when(j == 0)
    def _():
        xl_scr[0:HALO, :] = jnp.zeros((HALO, LRU_WIDTH), _F32)
        hcarry_scr[...] = jnp.zeros_like(hcarry_scr)

    x = x_ref[...]
    hb = _rms(x, n1g_ref[...]).astype(_BF16)

    x_lru = _dot(hb, w_in_ref[:, 2 * GM_WIDTH + LRU_WIDTH:])

    xl_scr[HALO:HALO + tm, :] = x_lru
    xr = cb_ref[...] + cw_ref[LRU_CONV - 1:LRU_CONV, :] * x_lru
    for k in range(LRU_CONV - 1):
        off = HALO - (LRU_CONV - 1) + k
        xr = xr + cw_ref[k:k + 1, :] * xl_scr[off:off + tm, :]
    xl_scr[0:HALO, :] = x_lru[tm - HALO:tm, :]
    z_v = _dot(hb, w_in_ref[:, GM_WIDTH:2 * GM_WIDTH])
    gz = [_dot(xr[:, h * HEAD_DIM:(h + 1) * HEAD_DIM].astype(_BF16), wgate_ref[h]) + bgate_ref[h]
          for h in range(HEADS)]

    v = jax.nn.gelu(z_v)
    mu = jnp.mean(v, axis=-1, keepdims=True)
    vc = v - mu
    v = vc * lax.rsqrt(jnp.mean(vc * vc, axis=-1, keepdims=True) + LN_EPS)
    vbf = (v * vg_ref[...] + vb_ref[...]).astype(_BF16)
    tri = (lax.broadcasted_iota(jnp.int32, (CHUNK, CHUNK), 0)
           >= lax.broadcasted_iota(jnp.int32, (CHUNK, CHUNK), 1))
    z_u = _dot(hb, w_in_ref[:, 0:GM_WIDTH])
    for h in range(HEADS):
        hs = slice(h * HEAD_DIM, (h + 1) * HEAD_DIM)
        w_h = jnp.where(tri, ws_ref[h], 0.0).astype(_BF16)
        for c in range(0, tm // CHUNK, 2):
            v2 = jnp.concatenate(
                [vbf[c * CHUNK:(c + 1) * CHUNK, hs], vbf[(c + 1) * CHUNK:(c + 2) * CHUNK, hs]],
                axis=1)
            m = _dot(w_h, v2)
            mix_scr[c * CHUNK:(c + 1) * CHUNK, hs] = m[:, :HEAD_DIM] + bsf_ref[:, hs]
            mix_scr[(c + 1) * CHUNK:(c + 2) * CHUNK, hs] = m[:, HEAD_DIM:] + bsf_ref[:, hs]
    g_lru = _dot(hb, w_in_ref[:, 2 * GM_WIDTH:2 * GM_WIDTH + LRU_WIDTH])

    y_gm = _rms(jax.nn.gelu(z_u) * mix_scr[...], gog_ref[...]).astype(_BF16)
    acc = x + _dot(y_gm, w_out_ref[0:GM_WIDTH, :])

    sp = jax.nn.softplus(-lam_ref[...])
    a_parts, b_parts = [], []
    for h in range(HEADS):
        hs = slice(h * HEAD_DIM, (h + 1) * HEAD_DIM)
        r = jax.nn.sigmoid(gz[h][:, :HEAD_DIM])
        i = jax.nn.sigmoid(gz[h][:, HEAD_DIM:])
        log_a = -LRU_C * r * sp[:, hs]
        a_parts.append(jnp.exp(log_a))
        t = jnp.tanh(log_a)
        b_parts.append(jnp.sqrt(-2.0 * t / (1.0 - t)) * (i * xr[:, hs]))
    a_all = jnp.concatenate(a_parts, axis=1)
    b_all = jnp.concatenate(b_parts, axis=1)

    row = lax.broadcasted_iota(jnp.int32, (SUBLANES, LRU_WIDTH), 0)
    h_prev = hcarry_scr[...]
    h_groups = []
    for g in range(tm // SUBLANES):
        a = a_all[g * SUBLANES:(g + 1) * SUBLANES, :]
        b = b_all[g * SUBLANES:(g + 1) * SUBLANES, :]
        for s in (1, 2, 4):
            keep = row >= s
            b = jnp.where(keep, a * pltpu.roll(b, s, 0) + b, b)
            a = jnp.where(keep, a * pltpu.roll(a, s, 0), a)
        hh = a * h_prev + b
        h_groups.append(hh)
        h_prev = hh[SUBLANES - 1:SUBLANES, :]
    hcarry_scr[...] = h_prev
    h_all = jnp.concatenate(h_groups, axis=0)

    y_lru = _rms(h_all * jax.nn.gelu(g_lru), log_ref[...]).astype(_BF16)
    o_ref[...] = acc + _dot(y_lru, w_out_ref[GM_WIDTH:, :])


def _ffn_kernel(n_tiles, x_hbm, n2g_ref, wg_ref, wv_ref, cw_ref, cb_ref,
                wd_ref, fg_ref, o_ref,
                x_scr, x_sem, h_scr, ug_scr, uv_scr, carry_g, carry_v):
    tm = FFN_TM
    j = pl.program_id(1)
    f = pl.program_id(2)
    tile = pl.program_id(0) * pl.num_programs(1) + j

    def x_copy(t):
        return pltpu.make_async_copy(x_hbm.at[pl.ds(t * tm, tm), :], x_scr, x_sem)

    @pl.when(f == 0)
    def _():
        @pl.when(tile == 0)
        def _():
            x_copy(tile).start()

        x_copy(tile).wait()
        x = x_scr[...]
        h_scr[...] = _rms(x, n2g_ref[...]).astype(_BF16)
        o_ref[...] = x

    @pl.when(jnp.logical_and(f == 1, tile + 1 < n_tiles))
    def _():
        x_copy(tile + 1).start()

    @pl.when(j == 0)
    def _():
        carry_g[f] = jnp.zeros((HALO, FFN_TF), _F32)
        carry_v[f] = jnp.zeros((HALO, FFN_TF), _F32)

    hb = h_scr[...]

    def conv(u, scr, carry, cw_ref, cb_ref):
        scr[0:HALO, :] = carry[f]
        scr[HALO:HALO + tm, :] = u
        carry[f] = u[tm - HALO:tm, :]
        y = cb_ref[...] + cw_ref[FFN_CONV - 1:FFN_CONV, :] * u
        for k in range(FFN_CONV - 1):
            off = HALO - (FFN_CONV - 1) + k
            y = y + cw_ref[k:k + 1, :] * scr[off:off + tm, :]
        return y

    gate = conv(_dot(hb, wg_ref[...]), ug_scr, carry_g, cw_ref.at[f], cb_ref.at[f])
    val = conv(_dot(hb, wv_ref[...]), uv_scr, carry_v, cw_ref.at[N_F + f], cb_ref.at[N_F + f])
    act = (jax.nn.gelu(gate) * val).astype(_BF16)
    o_ref[...] += _dot(act, wd_ref[...])

    @pl.when(f == N_F - 1)
    def _():
        o_ref[...] = _rms(o_ref[...], fg_ref[...])


def _const_spec(shape):
    nd = len(shape)
    return pl.BlockSpec(shape, lambda *_: (0,) * nd, pipeline_mode=pl.Buffered(1))


def _mixer(x2, n1g, w_in, vg, vb, ws, bsf, cw, cb, wgate, bgate, lam, gog, log, w_out,
           batch, seq):
    tm = MIX_TM
    nt = seq // tm
    row_spec = pl.BlockSpec((tm, D_MODEL), lambda b, j: (b * nt + j, 0))
    consts = (n1g, w_in, vg, vb, ws, bsf, cw, cb, wgate, bgate, lam, gog, log, w_out)
    return pl.pallas_call(
        _mixer_kernel,
        grid=(batch, nt),
        in_specs=[row_spec] + [_const_spec(c.shape) for c in consts],
        out_specs=row_spec,
        out_shape=jax.ShapeDtypeStruct(x2.shape, _F32),
        scratch_shapes=[
            pltpu.VMEM((HALO + tm, LRU_WIDTH), _F32),
            pltpu.VMEM((tm, GM_WIDTH), _F32),
            pltpu.VMEM((1, LRU_WIDTH), _F32),
        ],
        compiler_params=pltpu.CompilerParams(
            dimension_semantics=("arbitrary", "arbitrary"),
            vmem_limit_bytes=MIX_VMEM_LIMIT_BYTES),
        name="mixer",
    )(x2, *consts)


def _ffn(x2, n2g, w_up, cw, cb, w_down, fg, batch, seq):
    tm, tf = FFN_TM, FFN_TF
    nt = seq // tm
    out_spec = pl.BlockSpec((tm, D_MODEL), lambda b, j, f: (b * nt + j, 0))
    vec_spec = pl.BlockSpec((1, D_MODEL), lambda b, j, f: (0, 0))
    return pl.pallas_call(
        functools.partial(_ffn_kernel, batch * nt),
        grid=(batch, nt, N_F),
        in_specs=[
            pl.BlockSpec(memory_space=pl.ANY),
            vec_spec,
            pl.BlockSpec((D_MODEL, tf), lambda b, j, f: (0, f)),
            pl.BlockSpec((D_MODEL, tf), lambda b, j, f: (0, N_F + f)),
            _const_spec((2 * N_F, FFN_CONV, tf)),
            _const_spec((2 * N_F, 1, tf)),
            pl.BlockSpec((tf, D_MODEL), lambda b, j, f: (f, 0)),
            vec_spec,
        ],
        out_specs=out_spec,
        out_shape=jax.ShapeDtypeStruct(x2.shape, _F32),
        scratch_shapes=[
            pltpu.VMEM((tm, D_MODEL), _F32),
            pltpu.SemaphoreType.DMA(()),
            pltpu.VMEM((tm, D_MODEL), _BF16),
            pltpu.VMEM((HALO + tm, tf), _F32),
            pltpu.VMEM((HALO + tm, tf), _F32),
            pltpu.VMEM((N_F, HALO, tf), _F32),
            pltpu.VMEM((N_F, HALO, tf), _F32),
        ],
        compiler_params=pltpu.CompilerParams(
            dimension_semantics=("arbitrary", "arbitrary", "arbitrary"),
            vmem_limit_bytes=FFN_VMEM_LIMIT_BYTES),
        name="ffn",
    )(x2, n2g, w_up, w_up, cw, cb, w_down, fg)


def kernel(x, norm1_g, w_in, gm_v_g, gm_v_b, gm_ws, gm_bs, lru_conv_w, lru_conv_b, lru_wa, lru_ba, lru_wx, lru_bx, lru_lambda, gm_out_g, lru_out_g, w_out, norm2_g, ffn_w_up, ffn_conv_w, ffn_conv_b, ffn_w_down, final_g):
    batch, seq, d = x.shape
    assert w_in.shape[0] == 1, "single-layer block: the final norm is fused into the ffn call"
    x2 = x.reshape(batch * seq, d)
    bsf = jnp.repeat(gm_bs[0].T, HEAD_DIM, axis=1)
    wgate = jnp.concatenate([lru_wa[0], lru_wx[0]], axis=-1).astype(_BF16)
    bgate = jnp.concatenate([lru_ba[0], lru_bx[0]], axis=-1)[:, None, :]
    x2 = _mixer(
        x2, norm1_g, w_in[0].astype(_BF16), gm_v_g, gm_v_b, gm_ws[0], bsf,
        lru_conv_w[0], lru_conv_b, wgate, bgate, lru_lambda, gm_out_g, lru_out_g,
        w_out[0].astype(_BF16), batch, seq)
    cw = jnp.swapaxes(ffn_conv_w[0].reshape(FFN_CONV, 2 * N_F, FFN_TF), 0, 1)
    cb = ffn_conv_b[0].reshape(2 * N_F, 1, FFN_TF)
    x2 = _ffn(
        x2, norm2_g, ffn_w_up[0].astype(_BF16), cw, cb,
        ffn_w_down[0].astype(_BF16), final_g[None], batch, seq)
    return x2.reshape(batch, seq, d)
```

```python
import functools

import jax
import jax.numpy as jnp
from jax import lax
from jax.experimental import pallas as pl
from jax.experimental.pallas import tpu as pltpu

D_MODEL = 2048
GM_WIDTH = 1024
LRU_WIDTH = 1024
CHUNK = 128
HEADS = 8
HEAD_DIM = 128
LRU_CONV = 4
LRU_C = 8.0
D_FF = 6144
FFN_CONV = 3
RMS_EPS = 1e-6
LN_EPS = 1e-5

SUBLANES = 8
HALO = SUBLANES
MIX_VMEM_LIMIT_BYTES = 60 * 1024 * 1024
FFN_VMEM_LIMIT_BYTES = 62 * 1024 * 1024

MIX_TM = 512
FFN_TM = 1024
FFN_TF = 768
N_F = D_FF // FFN_TF

_BF16 = jnp.bfloat16
_F32 = jnp.float32


def _rms(x, g):
    return x * lax.rsqrt(jnp.mean(x * x, axis=-1, keepdims=True) + RMS_EPS) * g


def _dot(a, b):
    return jnp.dot(a, b, preferred_element_type=_F32)


def _mixer_kernel(x_ref, n1g_ref, w_in_ref, vg_ref, vb_ref, ws_ref, bsf_ref,
                  cw_ref, cb_ref, wgate_ref, bgate_ref, lam_ref, gog_ref, log_ref,
                  w_out_ref, o_ref,
                  xl_scr, mix_scr, hcarry_scr):
    tm = MIX_TM
    j = pl.program_id(1)

    @pl.when(j == 0)
    def _():
        xl_scr[0:HALO, :] = jnp.zeros((HALO, LRU_WIDTH), _F32)
        hcarry_scr[...] = jnp.zeros_like(hcarry_scr)

    x = x_ref[...]
    hb = _rms(x, n1g_ref[...]).astype(_BF16)

    x_lru = _dot(hb, w_in_ref[:, 2 * GM_WIDTH + LRU_WIDTH:])

    xl_scr[HALO:HALO + tm, :] = x_lru
    xr = cb_ref[...] + cw_ref[LRU_CONV - 1:LRU_CONV, :] * x_lru
    for k in range(LRU_CONV - 1):
        off = HALO - (LRU_CONV - 1) + k
        xr = xr + cw_ref[k:k + 1, :] * xl_scr[off:off + tm, :]
    xl_scr[0:HALO, :] = x_lru[tm - HALO:tm, :]
    z_v = _dot(hb, w_in_ref[:, GM_WIDTH:2 * GM_WIDTH])
    gz = [_dot(xr[:, h * HEAD_DIM:(h + 1) * HEAD_DIM].astype(_BF16), wgate_ref[h]) + bgate_ref[h]
          for h in range(HEADS)]

    v = jax.nn.gelu(z_v)
    mu = jnp.mean(v, axis=-1, keepdims=True)
    vc = v - mu
    v = vc * lax.rsqrt(jnp.mean(vc * vc, axis=-1, keepdims=True) + LN_EPS)
    vbf = (v * vg_ref[...] + vb_ref[...]).astype(_BF16)
    tri = (lax.broadcasted_iota(jnp.int32, (CHUNK, CHUNK), 0)
           >= lax.broadcasted_iota(jnp.int32, (CHUNK, CHUNK), 1))
    z_u = _dot(hb, w_in_ref[:, 0:GM_WIDTH])
    for h in range(HEADS):
        hs = slice(h * HEAD_DIM, (h + 1) * HEAD_DIM)
        w_h = jnp.where(tri, ws_ref[h], 0.0).astype(_BF16)
        for c in range(0, tm // CHUNK, 2):
            v2 = jnp.concatenate(
                [vbf[c * CHUNK:(c + 1) * CHUNK, hs], vbf[(c + 1) * CHUNK:(c + 2) * CHUNK, hs]],
                axis=1)
            m = _dot(w_h, v2)
            mix_scr[c * CHUNK:(c + 1) * CHUNK, hs] = m[:, :HEAD_DIM] + bsf_ref[:, hs]
            mix_scr[(c + 1) * CHUNK:(c + 2) * CHUNK, hs] = m[:, HEAD_DIM:] + bsf_ref[:, hs]
    g_lru = _dot(hb, w_in_ref[:, 2 * GM_WIDTH:2 * GM_WIDTH + LRU_WIDTH])

    y_gm = _rms(jax.nn.gelu(z_u) * mix_scr[...], gog_ref[...]).astype(_BF16)
    acc = x + _dot(y_gm, w_out_ref[0:GM_WIDTH, :])

    sp = jax.nn.softplus(-lam_ref[...])
    a_parts, b_parts = [], []
    for h in range(HEADS):
        hs = slice(h * HEAD_DIM, (h + 1) * HEAD_DIM)
        r = jax.nn.sigmoid(gz[h][:, :HEAD_DIM])
        i = jax.nn.sigmoid(gz[h][:, HEAD_DIM:])
        log_a = -LRU_C * r * sp[:, hs]
        a_parts.append(jnp.exp(log_a))
        t = jnp.tanh(log_a)
        b_parts.append(jnp.sqrt(-2.0 * t / (1.0 - t)) * (i * xr[:, hs]))
    a_all = jnp.concatenate(a_parts, axis=1)
    b_all = jnp.concatenate(b_parts, axis=1)

    row = lax.broadcasted_iota(jnp.int32, (SUBLANES, LRU_WIDTH), 0)
    h_prev = hcarry_scr[...]
    h_groups = []
    for g in range(tm // SUBLANES):
        a = a_all[g * SUBLANES:(g + 1) * SUBLANES, :]
        b = b_all[g * SUBLANES:(g + 1) * SUBLANES, :]
        for s in (1, 2, 4):
            keep = row >= s
            b = jnp.where(keep, a * pltpu.roll(b, s, 0) + b, b)
            a = jnp.where(keep, a * pltpu.roll(a, s, 0), a)
        hh = a * h_prev + b
        h_groups.append(hh)
        h_prev = hh[SUBLANES - 1:SUBLANES, :]
    hcarry_scr[...] = h_prev
    h_all = jnp.concatenate(h_groups, axis=0)

    y_lru = _rms(h_all * jax.nn.gelu(g_lru), log_ref[...]).astype(_BF16)
    o_ref[...] = acc + _dot(y_lru, w_out_ref[GM_WIDTH:, :])


def _ffn_kernel(n_tiles, x_hbm, n2g_ref, wg_ref, wv_ref, cw_ref, cb_ref,
                wd_ref, fg_ref, o_ref,
                x_scr, x_sem, h_scr, ug_scr, uv_scr, carry_g, carry_v):
    tm = FFN_TM
    j = pl.program_id(1)
    f = pl.program_id(2)
    tile = pl.program_id(0) * pl.num_programs(1) + j

    def x_copy(t):
        return pltpu.make_async_copy(x_hbm.at[pl.ds(t * tm, tm), :], x_scr, x_sem)

    @pl.when(f == 0)
    def _():
        @pl.when(tile == 0)
        def _():
            x_copy(tile).start()

        x_copy(tile).wait()
        h_scr[...] = _rms(x_scr[...], n2g_ref[...]).astype(_BF16)

    @pl.when(jnp.logical_and(f == 1, tile + 1 < n_tiles))
    def _():
        x_copy(tile + 1).start()

    @pl.when(j == 0)
    def _():
        carry_g[f] = jnp.zeros((HALO, FFN_TF), _F32)
        carry_v[f] = jnp.zeros((HALO, FFN_TF), _F32)

    def conv(u, scr, carry, cw_ref, cb_ref):
        scr[0:HALO, :] = carry[f]
        scr[HALO:HALO + tm, :] = u
        carry[f] = u[tm - HALO:tm, :]
        y = cb_ref[...] + cw_ref[FFN_CONV - 1:FFN_CONV, :] * u
        for k in range(FFN_CONV - 1):
            off = HALO - (FFN_CONV - 1) + k
            y = y + cw_ref[k:k + 1, :] * scr[off:off + tm, :]
        return y

    def down():
        hb = h_scr[...]
        gate = conv(_dot(hb, wg_ref[...]), ug_scr, carry_g, cw_ref.at[f], cb_ref.at[f])
        val = conv(_dot(hb, wv_ref[...]), uv_scr, carry_v, cw_ref.at[N_F + f], cb_ref.at[N_F + f])
        act = (jax.nn.gelu(gate) * val).astype(_BF16)
        return _dot(act, wd_ref[...])

    @pl.when(f == 0)
    def _():
        o_ref[...] = x_scr[...] + down()

    @pl.when(f > 0)
    def _():
        o_ref[...] += down()

    @pl.when(f == N_F - 1)
    def _():
        o_ref[...] = _rms(o_ref[...], fg_ref[...])


def _const_spec(shape):
    nd = len(shape)
    return pl.BlockSpec(shape, lambda *_: (0,) * nd, pipeline_mode=pl.Buffered(1))


def _mixer(x2, n1g, w_in, vg, vb, ws, bsf, cw, cb, wgate, bgate, lam, gog, log, w_out,
           batch, seq):
    tm = MIX_TM
    nt = seq // tm
    row_spec = pl.BlockSpec((tm, D_MODEL), lambda b, j: (b * nt + j, 0))
    consts = (n1g, w_in, vg, vb, ws, bsf, cw, cb, wgate, bgate, lam, gog, log, w_out)
    return pl.pallas_call(
        _mixer_kernel,
        grid=(batch, nt),
        in_specs=[row_spec] + [_const_spec(c.shape) for c in consts],
        out_specs=row_spec,
        out_shape=jax.ShapeDtypeStruct(x2.shape, _F32),
        scratch_shapes=[
            pltpu.VMEM((HALO + tm, LRU_WIDTH), _F32),
            pltpu.VMEM((tm, GM_WIDTH), _F32),
            pltpu.VMEM((1, LRU_WIDTH), _F32),
        ],
        compiler_params=pltpu.CompilerParams(
            dimension_semantics=("arbitrary", "arbitrary"),
            vmem_limit_bytes=MIX_VMEM_LIMIT_BYTES),
        name="mixer",
    )(x2, *consts)


def _ffn(x2, n2g, w_up, cw, cb, w_down, fg, batch, seq):
    tm, tf = FFN_TM, FFN_TF
    nt = seq // tm
    out_spec = pl.BlockSpec((tm, D_MODEL), lambda b, j, f: (b * nt + j, 0))
    vec_spec = pl.BlockSpec((1, D_MODEL), lambda b, j, f: (0, 0))
    return pl.pallas_call(
        functools.partial(_ffn_kernel, batch * nt),
        grid=(batch, nt, N_F),
        in_specs=[
            pl.BlockSpec(memory_space=pl.ANY),
            vec_spec,
            pl.BlockSpec((D_MODEL, tf), lambda b, j, f: (0, f)),
            pl.BlockSpec((D_MODEL, tf), lambda b, j, f: (0, N_F + f)),
            _const_spec((2 * N_F, FFN_CONV, tf)),
            _const_spec((2 * N_F, 1, tf)),
            pl.BlockSpec((tf, D_MODEL), lambda b, j, f: (f, 0)),
            vec_spec,
        ],
        out_specs=out_spec,
        out_shape=jax.ShapeDtypeStruct(x2.shape, _F32),
        scratch_shapes=[
            pltpu.VMEM((tm, D_MODEL), _F32),
            pltpu.SemaphoreType.DMA(()),
            pltpu.VMEM((tm, D_MODEL), _BF16),
            pltpu.VMEM((HALO + tm, tf), _F32),
            pltpu.VMEM((HALO + tm, tf), _F32),
            pltpu.VMEM((N_F, HALO, tf), _F32),
            pltpu.VMEM((N_F, HALO, tf), _F32),
        ],
        compiler_params=pltpu.CompilerParams(
            dimension_semantics=("arbitrary", "arbitrary", "arbitrary"),
            vmem_limit_bytes=FFN_VMEM_LIMIT_BYTES),
        name="ffn",
    )(x2, n2g, w_up, w_up, cw, cb, w_down, fg)


def kernel(x, norm1_g, w_in, gm_v_g, gm_v_b, gm_ws, gm_bs, lru_conv_w, lru_conv_b, lru_wa, lru_ba, lru_wx, lru_bx, lru_lambda, gm_out_g, lru_out_g, w_out, norm2_g, ffn_w_up, ffn_conv_w, ffn_conv_b, ffn_w_down, final_g):
    batch, seq, d = x.shape
    assert w_in.shape[0] == 1, "single-layer block: the final norm is fused into the ffn call"
    x2 = x.reshape(batch * seq, d)
    bsf = jnp.repeat(gm_bs[0].T, HEAD_DIM, axis=1)
    wgate = jnp.concatenate([lru_wa[0], lru_wx[0]], axis=-1).astype(_BF16)
    bgate = jnp.concatenate([lru_ba[0], lru_bx[0]], axis=-1)[:, None, :]
    x2 = _mixer(
        x2, norm1_g, w_in[0].astype(_BF16), gm_v_g, gm_v_b, gm_ws[0], bsf,
        lru_conv_w[0], lru_conv_b, wgate, bgate, lru_lambda, gm_out_g, lru_out_g,
        w_out[0].astype(_BF16), batch, seq)
    cw = jnp.swapaxes(ffn_conv_w[0].reshape(FFN_CONV, 2 * N_F, FFN_TF), 0, 1)
    cb = ffn_conv_b[0].reshape(2 * N_F, 1, FFN_TF)
    x2 = _ffn(
        x2, norm2_g, ffn_w_up[0].astype(_BF16), cw, cb,
        ffn_w_down[0].astype(_BF16), final_g[None], batch, seq)
    return x2.reshape(batch, seq, d)
```
